```python
import jax, jax.numpy as jnp
from jax import lax
import numpy as np

D_MODEL = 1024
BATCH = 4
SEQ = 4096
DEPTH = 4
DEC_BATCH = 128
DEC_SEQ = 4
PAST_LEN = 2048
PAGE_SIZE = 128

N_MIXERS = 2
N_HEADS = 8
HEAD_DIM = 128
N_KV_HEADS = 4
GROUP = N_HEADS // N_KV_HEADS
ROT_DIM = HEAD_DIM // 4
ROPE_THETA = 500000.0
IDX_HEADS = 8
IDX_DIM = 64
IDX_ROT_DIM = IDX_DIM // 4
TOPK_MAX = 256
Q_BLOCK = 128
EPS = 1e-6
FORGET_BIAS_INIT = 2.0
ATT_W = N_HEADS * HEAD_DIM
KV_W = N_KV_HEADS * HEAD_DIM
IN_A = 2 * ATT_W + 2 * KV_W + IDX_HEADS * IDX_DIM + IDX_DIM + IDX_HEADS
IN_B = 2 * ATT_W + 2 * KV_W + N_HEADS
N_LAYERS_A = (DEPTH + 1) // 2
N_LAYERS_B = DEPTH // 2

kernel_name = "hybrid_dsa_fox_decoder_step"


def rmsnorm(x, g):
    xf = x.astype(jnp.float32)
    y = xf * lax.rsqrt(jnp.mean(xf * xf, axis=-1, keepdims=True) + EPS)
    return (y * g.astype(jnp.float32)).astype(x.dtype)


def modulate(x, c, g, w_ada, b_ada):
    mod = c @ w_ada + b_ada
    shift, scale, gate = jnp.split(mod, 3, axis=-1)
    h = rmsnorm(x, g) * (1 + scale[:, None]) + shift[:, None]
    return h, gate[:, None]


def partial_rope(x, pos, rot_dim):
    half = rot_dim // 2
    inv = ROPE_THETA ** (-jnp.arange(half, dtype=jnp.float32) / half)
    ang = pos.astype(jnp.float32)[:, None] * inv[None, :]
    cos = jnp.cos(ang)[None, :, None, :]
    sin = jnp.sin(ang)[None, :, None, :]
    xr = x[..., :rot_dim].astype(jnp.float32)
    x1, x2 = xr[..., :half], xr[..., half:]
    rot = jnp.concatenate([x1 * cos - x2 * sin, x2 * cos + x1 * sin], axis=-1).astype(x.dtype)
    return jnp.concatenate([rot, x[..., rot_dim:]], axis=-1)


def split_cols(p, widths):
    offs = [int(o) for o in np.cumsum(widths)[:-1]]
    return jnp.split(p, offs, axis=-1)


def blocks(x):
    B, T = x.shape[:2]
    return jnp.moveaxis(x.reshape(B, T // Q_BLOCK, Q_BLOCK, *x.shape[2:]), 1, 0)


def unblocks(y):
    y = jnp.moveaxis(y, 0, 1)
    return y.reshape(y.shape[0], -1, *y.shape[3:])


def gather_pages(cache, page_table):
    g = cache[page_table]
    return g.reshape(g.shape[0], -1, *g.shape[3:])


def dsa_project(h, w_in, pos):
    B, T, _ = h.shape
    q, k, v, z, qi, ki, wi = split_cols(h @ w_in, [ATT_W, KV_W, KV_W, ATT_W, IDX_HEADS * IDX_DIM, IDX_DIM, IDX_HEADS])
    q = partial_rope(q.reshape(B, T, N_HEADS, HEAD_DIM), pos, ROT_DIM)
    k = partial_rope(k.reshape(B, T, N_KV_HEADS, HEAD_DIM), pos, ROT_DIM)
    v = v.reshape(B, T, N_KV_HEADS, HEAD_DIM)
    qi = partial_rope(qi.reshape(B, T, IDX_HEADS, IDX_DIM), pos, IDX_ROT_DIM)
    ki = partial_rope(ki.reshape(B, T, 1, IDX_DIM), pos, IDX_ROT_DIM)[:, :, 0]
    wi = wi * (IDX_HEADS ** -0.5 * IDX_DIM ** -0.5)
    return q, k, v, z, qi, ki, wi


def dsa_attend(q, qi, wi, q_pos, k_all, v_all, ki_all, key_pos, topk):
    B, Tq = q.shape[:2]
    causal = key_pos[None, :] <= q_pos[:, None]
    dots = jnp.einsum('bthd,bsd->bths', qi, ki_all, preferred_element_type=jnp.float32)
    score = jnp.einsum('bth,bths->bts', wi.astype(jnp.float32), jax.nn.relu(dots))
    score = jnp.where(causal[None], score, -jnp.inf)
    _, idx = lax.top_k(score, topk)
    take = jax.vmap(lambda a, i: a[i])
    k_sel = take(k_all, idx)
    v_sel = take(v_all, idx)
    valid = key_pos[idx] <= q_pos[None, :, None]
    qg = q.reshape(B, Tq, N_KV_HEADS, GROUP, HEAD_DIM)
    logits = jnp.einsum('btkgd,btskd->btkgs', qg, k_sel, preferred_element_type=jnp.float32) * (HEAD_DIM ** -0.5)
    logits = jnp.where(valid[:, :, None, None, :], logits, -jnp.inf)
    p = jax.nn.softmax(logits, axis=-1).astype(v_sel.dtype)
    o = jnp.einsum('btkgs,btskd->btkgd', p, v_sel)
    return o.reshape(B, Tq, ATT_W)


def dsa_prompt(h, w_in, w_out, pos):
    T = h.shape[1]
    q, k, v, z, qi, ki, wi = dsa_project(h, w_in, pos)
    topk = min(TOPK_MAX, T // 4)

    def blk(args):
        qb, qib, wib, pb = args
        return dsa_attend(qb, qib, wib, pb, k, v, ki, pos, topk)

    o = unblocks(lax.map(blk, (blocks(q), blocks(qi), blocks(wi), pos.reshape(-1, Q_BLOCK))))
    out = (o * jax.nn.silu(z)) @ w_out
    return out, k, v, ki


def dsa_sample(h, w_in, w_out, ck, cv, cki, page_table, pos, key_pos):
    q, k, v, z, qi, ki, wi = dsa_project(h, w_in, pos)
    k_all = jnp.concatenate([gather_pages(ck, page_table), k], axis=1)
    v_all = jnp.concatenate([gather_pages(cv, page_table), v], axis=1)
    ki_all = jnp.concatenate([gather_pages(cki, page_table), ki], axis=1)
    topk = min(TOPK_MAX, key_pos.shape[0] // 4)
    o = dsa_attend(q, qi, wi, pos, k_all, v_all, ki_all, key_pos, topk)
    out = (o * jax.nn.silu(z)) @ w_out
    return out, k, v, ki


def fox_project(h, w_in, b_f):
    B, T, _ = h.shape
    q, k, v, z, fl = split_cols(h @ w_in, [ATT_W, KV_W, KV_W, ATT_W, N_HEADS])
    q = q.reshape(B, T, N_HEADS, HEAD_DIM)
    k = k.reshape(B, T, N_KV_HEADS, HEAD_DIM)
    v = v.reshape(B, T, N_KV_HEADS, HEAD_DIM)
    logf = jax.nn.log_sigmoid((fl + b_f).astype(jnp.float32))
    return q, k, v, z, logf


def fox_attend(q, cq, q_pos, k_all, v_all, ck, key_pos):
    B, Tq = q.shape[:2]
    L = k_all.shape[1]
    qg = q.reshape(B, Tq, N_KV_HEADS, GROUP, HEAD_DIM)
    logits = jnp.einsum('btkgd,bskd->bkgts', qg, k_all, preferred_element_type=jnp.float32) * (HEAD_DIM ** -0.5)
    cq_ = jnp.transpose(cq.reshape(B, Tq, N_KV_HEADS, GROUP), (0, 2, 3, 1))[..., None]
    ck_ = jnp.transpose(ck.reshape(B, L, N_KV_HEADS, GROUP), (0, 2, 3, 1))[..., None, :]
    causal = key_pos[None, :] <= q_pos[:, None]
    logits = jnp.where(causal, logits + (cq_ - ck_), -jnp.inf)
    p = jax.nn.softmax(logits, axis=-1).astype(v_all.dtype)
    o = jnp.einsum('bkgts,bskd->btkgd', p, v_all)
    return o.reshape(B, Tq, ATT_W)


def fox_prompt(h, w_in, b_f, w_out, pos):
    q, k, v, z, logf = fox_project(h, w_in, b_f)
    c = jnp.cumsum(logf, axis=1)

    def blk(args):
        qb, cqb, pb = args
        return fox_attend(qb, cqb, pb, k, v, c, pos)

    o = unblocks(lax.map(blk, (blocks(q), blocks(c), pos.reshape(-1, Q_BLOCK))))
    out = (o * jax.nn.silu(z)) @ w_out
    return out, k, v, logf


def fox_sample(h, w_in, b_f, w_out, ck, cv, clogf, page_table, pos, key_pos):
    q, k, v, z, logf = fox_project(h, w_in, b_f)
    k_all = jnp.concatenate([gather_pages(ck, page_table), k], axis=1)
    v_all = jnp.concatenate([gather_pages(cv, page_table), v], axis=1)
    c_past = jnp.cumsum(gather_pages(clogf, page_table).astype(jnp.float32), axis=1)
    c_new = c_past[:, -1:] + jnp.cumsum(logf, axis=1)
    c_all = jnp.concatenate([c_past, c_new], axis=1)
    o = fox_attend(q, c_new, pos, k_all, v_all, c_all, key_pos)
    out = (o * jax.nn.silu(z)) @ w_out
    return out, k, v, logf


def setup_inputs(seed: int = 0) -> dict:
    key = jax.random.key(seed)
    ks = jax.random.split(key, 24)
    n_pages = PAST_LEN // PAGE_SIZE
    n_used = DEC_BATCH * n_pages
    n_pool = (5 * n_used) // 4
    f32 = jnp.float32
    nrm = lambda k, s: jax.random.normal(k, s, dtype=f32)
    page_table = jax.random.permutation(ks[0], n_pool)[:n_used].reshape(DEC_BATCH, n_pages).astype(jnp.int32)
    return {
        "x_prompt": nrm(ks[1], (BATCH, SEQ, D_MODEL)),
        "x_sample": nrm(ks[2], (DEC_BATCH, DEC_SEQ, D_MODEL)),
        "cache_dsa_k": nrm(ks[3], (N_LAYERS_A, n_pool, PAGE_SIZE, N_KV_HEADS, HEAD_DIM)),
        "cache_dsa_v": nrm(ks[4], (N_LAYERS_A, n_pool, PAGE_SIZE, N_KV_HEADS, HEAD_DIM)),
        "cache_dsa_kidx": nrm(ks[5], (N_LAYERS_A, n_pool, PAGE_SIZE, IDX_DIM)),
        "cache_fox_k": nrm(ks[6], (N_LAYERS_B, n_pool, PAGE_SIZE, N_KV_HEADS, HEAD_DIM)),
        "cache_fox_v": nrm(ks[7], (N_LAYERS_B, n_pool, PAGE_SIZE, N_KV_HEADS, HEAD_DIM)),
        "cache_fox_logf": jax.nn.log_sigmoid(FORGET_BIAS_INIT + nrm(ks[8], (N_LAYERS_B, n_pool, PAGE_SIZE, N_HEADS))),
        "page_table": page_table,
        "c_prompt": nrm(ks[9], (BATCH, D_MODEL)),
        "c_sample": nrm(ks[10], (DEC_BATCH, D_MODEL)),
        "norm_g": 1.0 + 0.1 * nrm(ks[11], (DEPTH, D_MODEL)),
        "w_ada": 0.5 * D_MODEL ** -0.5 * nrm(ks[12], (DEPTH, D_MODEL, 3 * D_MODEL)),
        "b_ada": 0.01 * nrm(ks[13], (DEPTH, 3 * D_MODEL)),
        "w_in_dsa": D_MODEL ** -0.5 * nrm(ks[14], (N_LAYERS_A, D_MODEL, IN_A)),
        "w_out_dsa": ATT_W ** -0.5 * nrm(ks[15], (N_LAYERS_A, ATT_W, D_MODEL)),
        "w_in_fox": D_MODEL ** -0.5 * nrm(ks[16], (N_LAYERS_B, D_MODEL, IN_B)),
        "b_forget": FORGET_BIAS_INIT + 0.1 * nrm(ks[17], (N_LAYERS_B, N_HEADS)),
        "w_out_fox": ATT_W ** -0.5 * nrm(ks[18], (N_LAYERS_B, ATT_W, D_MODEL)),
        "final_g": 1.0 + 0.1 * nrm(ks[19], (D_MODEL,)),
    }


def reference(x_prompt, x_sample, cache_dsa_k, cache_dsa_v, cache_dsa_kidx, cache_fox_k, cache_fox_v, cache_fox_logf,
              page_table, c_prompt, c_sample, norm_g, w_ada, b_ada, w_in_dsa, w_out_dsa, w_in_fox, b_forget,
              w_out_fox, final_g):
    t_p = x_prompt.shape[1]
    t_s = x_sample.shape[1]
    past_len = page_table.shape[1] * PAGE_SIZE
    pos_p = jnp.arange(t_p)
    pos_s = past_len + jnp.arange(t_s)
    key_pos_s = jnp.arange(past_len + t_s)

    xp, xs = x_prompt, x_sample
    dk_p, dv_p, dki_p, dk_s, dv_s, dki_s = [], [], [], [], [], []
    fk_p, fv_p, fl_p, fk_s, fv_s, fl_s = [], [], [], [], [], []
    for i in range(DEPTH):
        j = i // N_MIXERS
        hp, gp = modulate(xp, c_prompt, norm_g[i], w_ada[i], b_ada[i])
        hs, gs = modulate(xs, c_sample, norm_g[i], w_ada[i], b_ada[i])
        if i % N_MIXERS == 0:
            op, kp, vp, kip = dsa_prompt(hp, w_in_dsa[j], w_out_dsa[j], pos_p)
            osm, ksm, vsm, kism = dsa_sample(hs, w_in_dsa[j], w_out_dsa[j], cache_dsa_k[j], cache_dsa_v[j],
                                             cache_dsa_kidx[j], page_table, pos_s, key_pos_s)
            dk_p.append(kp); dv_p.append(vp); dki_p.append(kip)
            dk_s.append(ksm); dv_s.append(vsm); dki_s.append(kism)
        else:
            op, kp, vp, lfp = fox_prompt(hp, w_in_fox[j], b_forget[j], w_out_fox[j], pos_p)
            osm, ksm, vsm, lfs = fox_sample(hs, w_in_fox[j], b_forget[j], w_out_fox[j], cache_fox_k[j], cache_fox_v[j],
                                            cache_fox_logf[j], page_table, pos_s, key_pos_s)
            fk_p.append(kp); fv_p.append(vp); fl_p.append(lfp)
            fk_s.append(ksm); fv_s.append(vsm); fl_s.append(lfs)
        xp = xp + gp * op
        xs = xs + gs * osm

    y_prompt = rmsnorm(xp, final_g)
    y_sample = rmsnorm(xs, final_g)
    return (y_prompt, y_sample,
            jnp.stack(dk_p), jnp.stack(dv_p), jnp.stack(dki_p),
            jnp.stack(fk_p), jnp.stack(fv_p), jnp.stack(fl_p),
            jnp.stack(dk_s), jnp.stack(dv_s), jnp.stack(dki_s),
            jnp.stack(fk_s), jnp.stack(fv_s), jnp.stack(fl_s))
```

```python
import functools
import math

import jax
import jax.numpy as jnp
import numpy as np
from jax import lax
from jax.experimental import pallas as pl
from jax.experimental.pallas import tpu as pltpu

N_HEADS = 8
HEAD_DIM = 128
N_KV_HEADS = 4
GROUP = N_HEADS // N_KV_HEADS
ROT_DIM = HEAD_DIM // 4
ROPE_THETA = 500000.0
IDX_HEADS = 8
IDX_DIM = 64
IDX_ROT_DIM = IDX_DIM // 4
TOPK_MAX = 256
PAGE_SIZE = 128
EPS = 1e-6
ATT_W = N_HEADS * HEAD_DIM
KV_W = N_KV_HEADS * HEAD_DIM
N_MIXERS = 2

LANES = 128
SUBLANES = 8
VMEM_LIMIT = 56 * 1024 * 1024

SAMPLE_ROWS = SUBLANES
NEG = -1e30
INT_MIN = -(2 ** 31)

F32 = jnp.float32
BF16 = jnp.bfloat16
I32 = jnp.int32


def _params(n_axes):
    return pltpu.CompilerParams(dimension_semantics=("arbitrary",) * n_axes, vmem_limit_bytes=VMEM_LIMIT)


def _dot(a, b):
    return jnp.dot(a, b, preferred_element_type=F32)


def _dot_nt(a, b):
    return lax.dot_general(a, b, (((1,), (1,)), ((), ())), preferred_element_type=F32)


def _sigmoid(x):
    return 1.0 / (1.0 + jnp.exp(-x))


def _ada_kernel(c_ref, w_ref, b_ref, o_ref):
    c = c_ref[...].astype(BF16)
    w = w_ref[0].astype(BF16)
    o_ref[0] = _dot(c, w) + b_ref[0]


def _ada_call(c_all, w_ada, b_ada):
    depth, d, n3 = w_ada.shape
    m = c_all.shape[0]
    tn = d
    return pl.pallas_call(
        _ada_kernel,
        grid=(depth, n3 // tn),
        in_specs=[
            pl.BlockSpec((m, d), lambda l, j: (0, 0)),
            pl.BlockSpec((1, d, tn), lambda l, j: (l, 0, j)),
            pl.BlockSpec((1, 1, tn), lambda l, j: (l, 0, j)),
        ],
        out_specs=pl.BlockSpec((1, m, tn), lambda l, j: (l, 0, j)),
        out_shape=jax.ShapeDtypeStruct((depth, m, n3), F32),
        compiler_params=_params(2),
        name="ada_mod",
    )(c_all, w_ada, b_ada.reshape(depth, 1, n3))


def _modulated_input(x_ref, sc_ref, sh_ref, g_ref):
    x = x_ref[...]
    ms = jnp.mean(x * x, axis=-1, keepdims=True)
    y = x * lax.rsqrt(ms + EPS) * g_ref[...]
    return (y * (1.0 + sc_ref[0]) + sh_ref[0]).astype(BF16)


def _rope(v, lane, cos_t, sin_t, period, half):
    vr = jnp.where((lane & (period - 1)) < half, pltpu.roll(v, LANES - half, 1), pltpu.roll(v, half, 1))
    return v * cos_t + vr * sin_t


def _inproj_dsa_kernel(x_ref, sc_ref, sh_ref, g_ref, c128_ref, s128_ref, c64_ref, s64_ref, w_ref, wt_ref,
                       q_ref, kf_ref, kb_ref, vf_ref, vb_ref, gz_ref, qi_ref, kif_ref, kia_ref, kib_ref, wi_ref,
                       *, q_scale, wi_scale):
    h = _modulated_input(x_ref, sc_ref, sh_ref, g_ref)
    tm = h.shape[0]
    lane = lax.broadcasted_iota(I32, (tm, LANES), 1)
    c128, s128, c64, s64 = c128_ref[...], s128_ref[...], c64_ref[...], s64_ref[...]
    rope_h = lambda v: _rope(v, lane, c128, s128, HEAD_DIM, ROT_DIM // 2)
    rope_i = lambda v: _rope(v, lane, c64, s64, IDX_DIM, IDX_ROT_DIM // 2)
    seg = 4 * LANES
    off = 0
    for s in range(ATT_W // seg):
        r = _dot(h, w_ref[:, off:off + seg])
        for j in range(seg // LANES):
            c0 = s * seg + j * LANES
            q_ref[:, c0:c0 + LANES] = (rope_h(r[:, j * LANES:(j + 1) * LANES]) * q_scale).astype(BF16)
        off += seg
    r = _dot(h, w_ref[:, off:off + KV_W])
    for j in range(N_KV_HEADS):
        kr = rope_h(r[:, j * LANES:(j + 1) * LANES])
        kf_ref[:, j * LANES:(j + 1) * LANES] = kr
        kb_ref[:, j * LANES:(j + 1) * LANES] = kr.astype(BF16)
    off += KV_W
    r = _dot(h, w_ref[:, off:off + KV_W])
    vf_ref[...] = r
    vb_ref[...] = r.astype(BF16)
    off += KV_W
    for s in range(ATT_W // seg):
        r = _dot(h, w_ref[:, off:off + seg])
        gz_ref[:, s * seg:(s + 1) * seg] = (r * _sigmoid(r)).astype(BF16)
        off += seg
    r = _dot(h, w_ref[:, off:off + IDX_HEADS * IDX_DIM])
    for j in range(IDX_HEADS * IDX_DIM // LANES):
        qi_ref[:, j * LANES:(j + 1) * LANES] = rope_i(r[:, j * LANES:(j + 1) * LANES]).astype(BF16)
    t = _dot(h, wt_ref[...])
    tr = rope_i(t)
    kif_ref[...] = tr[:, 0:IDX_DIM]
    ka = jnp.where(lane < IDX_DIM, tr, 0.0)
    kia_ref[...] = ka.astype(BF16)
    kib_ref[...] = pltpu.roll(ka, IDX_DIM, 1).astype(BF16)
    wi_ref[...] = t[:, IDX_DIM:IDX_DIM + IDX_HEADS] * wi_scale


def _inproj_fox_kernel(x_ref, sc_ref, sh_ref, g_ref, w_ref, wfl_ref, bf_ref,
                       q_ref, kf_ref, kb_ref, vf_ref, vb_ref, gz_ref, lft_ref, *, q_scale):
    h = _modulated_input(x_ref, sc_ref, sh_ref, g_ref)
    seg = 4 * LANES
    off = 0
    for s in range(ATT_W // seg):
        r = _dot(h, w_ref[:, off:off + seg])
        q_ref[:, s * seg:(s + 1) * seg] = (r * q_scale).astype(BF16)
        off += seg
    r = _dot(h, w_ref[:, off:off + KV_W])
    kf_ref[...] = r
    kb_ref[...] = r.astype(BF16)
    off += KV_W
    r = _dot(h, w_ref[:, off:off + KV_W])
    vf_ref[...] = r
    vb_ref[...] = r.astype(BF16)
    off += KV_W
    for s in range(ATT_W // seg):
        r = _dot(h, w_ref[:, off:off + seg])
        gz_ref[:, s * seg:(s + 1) * seg] = (r * _sigmoid(r)).astype(BF16)
        off += seg
    z = _dot_nt(wfl_ref[...], h)[0:N_HEADS, :] + bf_ref[...]
    lft_ref[...] = jnp.minimum(z, 0.0) - jnp.log1p(jnp.exp(-jnp.abs(z)))


def _row_tile(rows):
    return 512 if rows % 512 == 0 else rows


def _mod_specs(tm, d, tiles_per_batch, per_row):
    if per_row:
        return pl.BlockSpec((1, tm, d), lambda i: (i, 0, 0))
    return pl.BlockSpec((1, 1, d), lambda i: (i // tiles_per_batch, 0, 0))


def _inproj_dsa_call(x, scale, shift, g, tabs, w_main, w_tail, *, tiles_per_batch, per_row):
    rows, d = x.shape
    tm = _row_tile(rows) if per_row else rows // (scale.shape[0] * tiles_per_batch)
    n_t = tabs[0].shape[0] // tm
    row = lambda w: pl.BlockSpec((tm, w), lambda i: (i, 0))
    tab = pl.BlockSpec((tm, LANES), lambda i: (i % n_t, 0))
    mod = _mod_specs(tm, d, tiles_per_batch, per_row)
    full = lambda a: pl.BlockSpec(a.shape, lambda i: (0,) * a.ndim)
    outs = [(ATT_W, BF16), (KV_W, F32), (KV_W, BF16), (KV_W, F32), (KV_W, BF16), (ATT_W, BF16),
            (IDX_HEADS * IDX_DIM, BF16), (IDX_DIM, F32), (LANES, BF16), (LANES, BF16), (IDX_HEADS, F32)]
    return pl.pallas_call(
        functools.partial(_inproj_dsa_kernel, q_scale=HEAD_DIM ** -0.5,
                          wi_scale=IDX_HEADS ** -0.5 * IDX_DIM ** -0.5),
        grid=(rows // tm,),
        in_specs=[row(d), mod, mod, full(g), tab, tab, tab, tab, full(w_main), full(w_tail)],
        out_specs=[row(w) for w, _ in outs],
        out_shape=[jax.ShapeDtypeStruct((rows, w), t) for w, t in outs],
        compiler_params=_params(1),
        name="inproj_dsa",
    )(x, scale, shift, g, *tabs, w_main, w_tail)


def _inproj_fox_call(x, scale, shift, g, w_main, w_flt, b_f, *, tiles_per_batch, per_row):
    rows, d = x.shape
    tm = _row_tile(rows) if per_row else rows // (scale.shape[0] * tiles_per_batch)
    row = lambda w: pl.BlockSpec((tm, w), lambda i: (i, 0))
    mod = _mod_specs(tm, d, tiles_per_batch, per_row)
    full = lambda a: pl.BlockSpec(a.shape, lambda i: (0,) * a.ndim)
    outs = [(ATT_W, BF16), (KV_W, F32), (KV_W, BF16), (KV_W, F32), (KV_W, BF16), (ATT_W, BF16)]
    return pl.pallas_call(
        functools.partial(_inproj_fox_kernel, q_scale=HEAD_DIM ** -0.5),
        grid=(rows // tm,),
        in_specs=[row(d), mod, mod, full(g), full(w_main), full(w_flt), full(b_f)],
        out_specs=[row(w) for w, _ in outs] + [pl.BlockSpec((N_HEADS, tm), lambda i: (0, i))],
        out_shape=[jax.ShapeDtypeStruct((rows, w), t) for w, t in outs]
        + [jax.ShapeDtypeStruct((N_HEADS, rows), F32)],
        compiler_params=_params(1),
        name="inproj_fox",
    )(x, scale, shift, g, w_main, w_flt, b_f)


def _outproj_kernel(og_ref, x_ref, gate_ref, w_ref, fg_ref, o_ref, *, final):
    xn = x_ref[...] + gate_ref[0] * _dot(og_ref[...], w_ref[...])
    if final:
        ms = jnp.mean(xn * xn, axis=-1, keepdims=True)
        xn = xn * lax.rsqrt(ms + EPS) * fg_ref[...]
    o_ref[...] = xn


def _outproj_call(og, x, gate, w_out, final_g, *, tiles_per_batch, per_row, final):
    rows, d = x.shape
    tm = _row_tile(rows) if per_row else rows // (gate.shape[0] * tiles_per_batch)
    row = lambda w: pl.BlockSpec((tm, w), lambda i: (i, 0))
    full = lambda a: pl.BlockSpec(a.shape, lambda i: (0,) * a.ndim)
    return pl.pallas_call(
        functools.partial(_outproj_kernel, final=final),
        grid=(rows // tm,),
        in_specs=[row(ATT_W), row(d), _mod_specs(tm, d, tiles_per_batch, per_row), full(w_out), full(final_g)],
        out_specs=row(d),
        out_shape=jax.ShapeDtypeStruct((rows, d), F32),
        compiler_params=_params(1),
        name="outproj",
    )(og, x, gate, w_out, final_g)


def _lane_cumsum(x):
    n = x.shape[-1]
    lane = lax.broadcasted_iota(I32, x.shape, x.ndim - 1)
    s = 1
    while s < n:
        x = x + jnp.where(lane >= s, pltpu.roll(x, s, x.ndim - 1), 0.0)
        s *= 2
    return x


def _cumsum_kernel(x_ref, o_ref):
    o_ref[...] = _lane_cumsum(x_ref[...])


def _cumsum_call(lft, n_batch):
    h, rows = lft.shape
    t = rows // n_batch
    return pl.pallas_call(
        _cumsum_kernel,
        grid=(n_batch,),
        in_specs=[pl.BlockSpec((h, t), lambda b: (0, b))],
        out_specs=pl.BlockSpec((h, t), lambda b: (0, b)),
        out_shape=jax.ShapeDtypeStruct((h, rows), F32),
        compiler_params=_params(1),
        name="logf_cumsum",
    )(lft)


def _sort_key(score):
    bits = pltpu.bitcast(score, I32)
    key = bits ^ ((bits >> 31) & 0x7FFFFFFF)
    return jnp.where(score == 0.0, 0, key)


def _kth_largest(count_ge, k):
    kf = float(k)
    p0 = jnp.where(count_ge(jnp.int32(0)) >= kf, 0, INT_MIN).astype(I32)

    def body(it, prefix):
        cand = prefix + lax.shift_left(jnp.int32(1), 30 - it)
        return jnp.where(count_ge(cand) >= kf, cand, prefix)

    return lax.fori_loop(0, 31, body, p0)


def _tie_cut(count_tie_below, need, n_bits):
    def body(it, j):
        cand = j + lax.shift_left(jnp.int32(1), n_bits - 1 - it)
        return jnp.where(count_tie_below(cand) < need, cand, j)

    return lax.fori_loop(0, n_bits, body, jnp.zeros(need.shape, I32))


def _dsa_prompt_kernel(q_ref, qi_ref, wi_ref, gz_ref, k_ref, v_ref, kia_ref, kib_ref, o_ref, s_scr, b_scr,
                       *, tq, ck, topk, n_bits):
    i = pl.program_id(1)
    n_ck = ((i + 1) * tq + ck - 1) // ck
    row = i * tq + lax.broadcasted_iota(I32, (tq, ck), 0)
    lane = lax.broadcasted_iota(I32, (tq, ck), 1)
    wi = wi_ref[...]

    def score_chunk(c, carry):
        ka = kia_ref[pl.ds(c * ck, ck), :]
        kb = kib_ref[pl.ds(c * ck, ck), :]
        acc = None
        for p in range(IDX_HEADS // 2):
            slab = qi_ref[:, p * LANES:(p + 1) * LANES]
            for u, kk in enumerate((ka, kb)):
                hh = 2 * p + u
                term = jnp.maximum(_dot_nt(slab, kk), 0.0) * wi[:, hh:hh + 1]
                acc = term if acc is None else acc + term
        col = c * ck + lane
        s_scr[c] = jnp.where(col <= row, _sort_key(acc), INT_MIN)
        return carry

    lax.fori_loop(0, n_ck, score_chunk, 0)

    def count(pred):
        def body(c, acc):
            m = jnp.where(pred(s_scr[c], c * ck + lane), 1.0, 0.0)
            for u in range(ck // LANES):
                acc = acc + m[:, u * LANES:(u + 1) * LANES]
            return acc

        acc = lax.fori_loop(0, n_ck, body, jnp.zeros((tq, LANES), F32))
        return jnp.sum(acc, axis=1, keepdims=True)

    v = _kth_largest(lambda cand: count(lambda key, col: key >= cand), topk)
    n_gt = count(lambda key, col: key > v)
    n_ge = count(lambda key, col: key >= v)
    need = float(topk) - n_gt
    has_tie = jnp.max(jnp.where((n_ge > float(topk)) & (v != INT_MIN), 1.0, 0.0)) > 0.5
    jcut = lax.cond(
        has_tie,
        lambda: _tie_cut(lambda cand: count(lambda key, col: (key == v) & (col < cand)), need, n_bits),
        lambda: jnp.full((tq, 1), 2 ** n_bits, I32))

    def bias_chunk(c, carry):
        key = s_scr[c]
        col = c * ck + lane
        sel = ((key > v) | ((key == v) & (col <= jcut))) & (key != INT_MIN)
        b_scr[c] = jnp.where(sel, 0.0, NEG)
        return carry

    lax.fori_loop(0, n_ck, bias_chunk, 0)

    for kv in range(N_KV_HEADS):
        h0 = kv * GROUP
        q2 = jnp.concatenate([q_ref[:, (h0 + g) * LANES:(h0 + g + 1) * LANES] for g in range(GROUP)], axis=0)

        def att_chunk(c, carry, kv=kv, q2=q2):
            m, l, acc = carry
            kc = k_ref[pl.ds(c * ck, ck), kv * LANES:(kv + 1) * LANES]
            vc = v_ref[pl.ds(c * ck, ck), kv * LANES:(kv + 1) * LANES]
            b = b_scr[c]
            s = _dot_nt(q2, kc) + jnp.concatenate([b] * GROUP, axis=0)
            mn = jnp.maximum(m, jnp.max(s, axis=1, keepdims=True))
            a = jnp.exp(m - mn)
            p = jnp.exp(s - mn)
            l = a * l + jnp.sum(p, axis=1, keepdims=True)
            acc = a * acc + _dot(p.astype(BF16), vc)
            return mn, l, acc

        init = (jnp.full((GROUP * tq, 1), NEG, F32), jnp.zeros((GROUP * tq, 1), F32),
                jnp.zeros((GROUP * tq, LANES), F32))
        _, l, acc = lax.fori_loop(0, n_ck, att_chunk, init)
        o = acc / l
        for g in range(GROUP):
            c0 = (h0 + g) * LANES
            o_ref[:, c0:c0 + LANES] = (o[g * tq:(g + 1) * tq] * gz_ref[:, c0:c0 + LANES].astype(F32)).astype(BF16)


def _dsa_prompt_call(q, qi, wi, gz, kb, vb, kia, kib, *, n_batch):
    rows = q.shape[0]
    t = rows // n_batch
    tq = min(128, t)
    ck = min(512, t)
    n_q = t // tq
    topk = min(TOPK_MAX, t // 4)
    n_bits = max(1, int(math.ceil(math.log2(t))))
    qrow = lambda w: pl.BlockSpec((tq, w), lambda b, i: (b * n_q + i, 0))
    seq = lambda w: pl.BlockSpec((t, w), lambda b, i: (b, 0))
    return pl.pallas_call(
        functools.partial(_dsa_prompt_kernel, tq=tq, ck=ck, topk=topk, n_bits=n_bits),
        grid=(n_batch, n_q),
        in_specs=[qrow(ATT_W), qrow(IDX_HEADS * IDX_DIM), qrow(IDX_HEADS), qrow(ATT_W),
                  seq(KV_W), seq(KV_W), seq(LANES), seq(LANES)],
        out_specs=qrow(ATT_W),
        out_shape=jax.ShapeDtypeStruct((rows, ATT_W), BF16),
        scratch_shapes=[pltpu.VMEM((t // ck, tq, ck), I32), pltpu.VMEM((t // ck, tq, ck), F32)],
        compiler_params=_params(2),
        name="dsa_prompt_attn",
    )(q, qi, wi, gz, kb, vb, kia, kib)


def _fox_prompt_kernel(q_ref, gz_ref, k_ref, v_ref, c_ref, o_ref, *, tq):
    i = pl.program_id(1)
    row = lax.broadcasted_iota(I32, (tq, tq), 0)
    col = lax.broadcasted_iota(I32, (tq, tq), 1)
    causal = col <= row
    for kv in range(N_KV_HEADS):
        h0 = kv * GROUP
        q2 = jnp.concatenate([q_ref[:, (h0 + g) * LANES:(h0 + g + 1) * LANES] for g in range(GROUP)], axis=0)

        def chunk(j, carry, diagonal, kv=kv, q2=q2, h0=h0):
            m, l, acc = carry
            kc = k_ref[pl.ds(j * tq, tq), kv * LANES:(kv + 1) * LANES]
            vc = v_ref[pl.ds(j * tq, tq), kv * LANES:(kv + 1) * LANES]
            cj = c_ref[0, j]
            s = _dot_nt(q2, kc)
            parts = []
            for g in range(GROUP):
                sg = s[g * tq:(g + 1) * tq] - cj[h0 + g:h0 + g + 1, :]
                if diagonal:
                    sg = jnp.where(causal, sg, NEG)
                parts.append(sg)
            s = jnp.concatenate(parts, axis=0)
            mn = jnp.maximum(m, jnp.max(s, axis=1, keepdims=True))
            a = jnp.exp(m - mn)
            p = jnp.exp(s - mn)
            l = a * l + jnp.sum(p, axis=1, keepdims=True)
            acc = a * acc + _dot(p.astype(BF16), vc)
            return mn, l, acc

        init = (jnp.full((GROUP * tq, 1), NEG, F32), jnp.zeros((GROUP * tq, 1), F32),
                jnp.zeros((GROUP * tq, LANES), F32))
        carry = lax.fori_loop(0, i, functools.partial(chunk, diagonal=False), init)
        _, l, acc = chunk(i, carry, True)
        o = acc / l
        for g in range(GROUP):
            c0 = (h0 + g) * LANES
            o_ref[:, c0:c0 + LANES] = (o[g * tq:(g + 1) * tq] * gz_ref[:, c0:c0 + LANES].astype(F32)).astype(BF16)


def _fox_prompt_call(q, gz, kb, vb, c_chunks, *, n_batch):
    rows = q.shape[0]
    t = rows // n_batch
    tq = c_chunks.shape[-1]
    n_q = t // tq
    qrow = lambda w: pl.BlockSpec((tq, w), lambda b, i: (b * n_q + i, 0))
    seq = lambda w: pl.BlockSpec((t, w), lambda b, i: (b, 0))
    return pl.pallas_call(
        functools.partial(_fox_prompt_kernel, tq=tq),
        grid=(n_batch, n_q),
        in_specs=[qrow(ATT_W), qrow(ATT_W), seq(KV_W), seq(KV_W),
                  pl.BlockSpec((1, n_q, N_HEADS, tq), lambda b, i: (b, 0, 0, 0))],
        out_specs=qrow(ATT_W),
        out_shape=jax.ShapeDtypeStruct((rows, ATT_W), BF16),
        compiler_params=_params(2),
        name="fox_prompt_attn",
    )(q, gz, kb, vb, c_chunks)


def _group_rows(x0, x1, row):
    return jnp.where(row < SAMPLE_ROWS // 2, x0, pltpu.roll(x1, SAMPLE_ROWS // 2, 0))


def _page_block(new_ref):
    new = new_ref[0]
    return jnp.concatenate([new, jnp.zeros((PAGE_SIZE - SAMPLE_ROWS, new.shape[1]), new.dtype)], axis=0)


def _sample_attend(q_ref, gz_ref, k_pages, v_pages, bias_fn, o_ref):
    row = lax.broadcasted_iota(I32, (SAMPLE_ROWS, LANES), 0)
    real = row < SAMPLE_ROWS // 2
    qf = q_ref[0].astype(F32)
    gz = gz_ref[0].astype(F32)
    for kv in range(N_KV_HEADS):
        h0 = kv * GROUP
        sl = lambda a, h: a[:, h * LANES:(h + 1) * LANES]
        q2 = _group_rows(sl(qf, h0), sl(qf, h0 + 1), row).astype(BF16)
        s = jnp.concatenate([_dot_nt(q2, kp[:, kv * LANES:(kv + 1) * LANES]) for kp in k_pages], axis=1)
        s = s + bias_fn(kv)
        m = jnp.max(s, axis=1, keepdims=True)
        p = jnp.exp(s - m)
        l = jnp.sum(p, axis=1, keepdims=True)
        pb = p.astype(BF16)
        o = None
        for n, vp in enumerate(v_pages):
            t = _dot(pb[:, n * LANES:(n + 1) * LANES], vp[:, kv * LANES:(kv + 1) * LANES])
            o = t if o is None else o + t
        o = o / l
        o_ref[0, :, h0 * LANES:(h0 + 1) * LANES] = jnp.where(real, o * sl(gz, h0), 0.0).astype(BF16)
        o_ref[0, :, (h0 + 1) * LANES:(h0 + 2) * LANES] = jnp.where(
            real, pltpu.roll(o, SAMPLE_ROWS // 2, 0) * sl(gz, h0 + 1), 0.0).astype(BF16)


def _dsa_sample_kernel(pt_ref, q_ref, qi_ref, wi_ref, gz_ref, kn_ref, vn_ref, kin_ref, *rest,
                       n_pages, topk, n_bits):
    k_refs, v_refs, ki_refs = rest[:n_pages], rest[n_pages:2 * n_pages], rest[2 * n_pages:3 * n_pages]
    o_ref = rest[3 * n_pages]
    n_chunks = n_pages + 1
    n_keys = n_chunks * PAGE_SIZE
    past = n_pages * PAGE_SIZE
    qi = qi_ref[0]
    wi = wi_ref[0]

    def score(ki_bf):
        t = jnp.maximum(_dot_nt(qi, ki_bf), 0.0) * wi
        return jnp.sum(t.reshape(SAMPLE_ROWS, IDX_HEADS, PAGE_SIZE), axis=1)

    chunks = [score(r[0, 0].astype(BF16)) for r in ki_refs] + [score(_page_block(kin_ref).astype(BF16))]
    score_all = jnp.concatenate(chunks, axis=1)
    row = lax.broadcasted_iota(I32, (SAMPLE_ROWS, n_keys), 0)
    col = lax.broadcasted_iota(I32, (SAMPLE_ROWS, n_keys), 1)
    causal = col <= past + row
    key = jnp.where(causal, _sort_key(score_all), INT_MIN)

    count = lambda m: jnp.sum(jnp.where(m, 1.0, 0.0), axis=1, keepdims=True)
    v = _kth_largest(lambda cand: count(key >= cand), topk)
    n_gt = count(key > v)
    n_ge = count(key >= v)
    need = float(topk) - n_gt
    has_tie = jnp.max(jnp.where((n_ge > float(topk)) & (v != INT_MIN), 1.0, 0.0)) > 0.5
    jcut = lax.cond(
        has_tie,
        lambda: _tie_cut(lambda cand: count((key == v) & (col < cand)), need, n_bits),
        lambda: jnp.full((SAMPLE_ROWS, 1), 2 ** n_bits, I32))
    sel = ((key > v) | ((key == v) & (col <= jcut))) & causal
    bias = jnp.where(sel, 0.0, NEG)
    bias2 = jnp.where(row < SAMPLE_ROWS // 2, bias, pltpu.roll(bias, SAMPLE_ROWS // 2, 0))

    k_pages = [r[0, 0].astype(BF16) for r in k_refs] + [_page_block(kn_ref).astype(BF16)]
    v_pages = [r[0, 0].astype(BF16) for r in v_refs] + [_page_block(vn_ref).astype(BF16)]
    _sample_attend(q_ref, gz_ref, k_pages, v_pages, lambda kv: bias2, o_ref)


def _fox_sample_kernel(pt_ref, q_ref, gz_ref, kn_ref, vn_ref, lfn_ref, *rest, n_pages):
    k_refs, v_refs, lf_refs = rest[:n_pages], rest[n_pages:2 * n_pages], rest[2 * n_pages:3 * n_pages]
    o_ref = rest[3 * n_pages]
    n_keys = (n_pages + 1) * PAGE_SIZE
    past = n_pages * PAGE_SIZE
    eye = (lax.broadcasted_iota(I32, (N_HEADS, N_HEADS), 0) == lax.broadcasted_iota(I32, (N_HEADS, N_HEADS), 1))
    eye = eye.astype(F32)
    lf_t = [lax.dot_general(eye, r[0, 0], (((1,), (1,)), ((), ())), precision=lax.Precision.HIGHEST,
                            preferred_element_type=F32) for r in lf_refs]
    c_all = _lane_cumsum(jnp.concatenate(lf_t + [lfn_ref[0]], axis=1))
    row = lax.broadcasted_iota(I32, (SAMPLE_ROWS, n_keys), 0)
    col = lax.broadcasted_iota(I32, (SAMPLE_ROWS, n_keys), 1)
    tok = row & (SAMPLE_ROWS // 2 - 1)
    mask = jnp.where(col <= past + tok, 0.0, NEG)

    def bias_fn(kv):
        h0 = kv * GROUP
        c2 = jnp.where(row < SAMPLE_ROWS // 2, c_all[h0:h0 + 1, :], c_all[h0 + 1:h0 + 2, :])
        return mask - c2

    k_pages = [r[0, 0].astype(BF16) for r in k_refs] + [_page_block(kn_ref).astype(BF16)]
    v_pages = [r[0, 0].astype(BF16) for r in v_refs] + [_page_block(vn_ref).astype(BF16)]
    _sample_attend(q_ref, gz_ref, k_pages, v_pages, bias_fn, o_ref)


def _page_specs(layer, n_pages, width):
    def spec(p):
        return pl.BlockSpec((1, 1, PAGE_SIZE, width), lambda b, pt: (layer, pt[b * n_pages + p], 0, 0))
    return [spec(p) for p in range(n_pages)]


def _seq_spec(a):
    return pl.BlockSpec((1,) + a.shape[1:], lambda b, pt: (b,) + (0,) * (a.ndim - 1))


def _dsa_sample_call(pt_flat, q, qi, wi, gz, kn, vn, kin, cache_k, cache_v, cache_ki, *, layer, n_pages, dec_seq):
    n_seq = q.shape[0]
    n_keys = (n_pages + 1) * PAGE_SIZE
    topk = min(TOPK_MAX, (n_pages * PAGE_SIZE + dec_seq) // 4)
    n_bits = int(math.ceil(math.log2(n_keys)))
    seq_in = [q, qi, wi, gz, kn, vn, kin]
    grid_spec = pltpu.PrefetchScalarGridSpec(
        num_scalar_prefetch=1,
        grid=(n_seq,),
        in_specs=[_seq_spec(a) for a in seq_in]
        + _page_specs(layer, n_pages, KV_W) + _page_specs(layer, n_pages, KV_W) + _page_specs(layer, n_pages, IDX_DIM),
        out_specs=pl.BlockSpec((1, SAMPLE_ROWS, ATT_W), lambda b, pt: (b, 0, 0)),
    )
    return pl.pallas_call(
        functools.partial(_dsa_sample_kernel, n_pages=n_pages, topk=topk, n_bits=n_bits),
        grid_spec=grid_spec,
        out_shape=jax.ShapeDtypeStruct((n_seq, SAMPLE_ROWS, ATT_W), BF16),
        compiler_params=_params(1),
        name="dsa_sample_attn",
    )(pt_flat, *seq_in, *([cache_k] * n_pages), *([cache_v] * n_pages), *([cache_ki] * n_pages))


def _fox_sample_call(pt_flat, q, gz, kn, vn, lfn, cache_k, cache_v, cache_lf, *, layer, n_pages):
    n_seq = q.shape[0]
    seq_in = [q, gz, kn, vn, lfn]
    grid_spec = pltpu.PrefetchScalarGridSpec(
        num_scalar_prefetch=1,
        grid=(n_seq,),
        in_specs=[_seq_spec(a) for a in seq_in]
        + _page_specs(layer, n_pages, KV_W) + _page_specs(layer, n_pages, KV_W) + _page_specs(layer, n_pages, N_HEADS),
        out_specs=pl.BlockSpec((1, SAMPLE_ROWS, ATT_W), lambda b, pt: (b, 0, 0)),
    )
    return pl.pallas_call(
        functools.partial(_fox_sample_kernel, n_pages=n_pages),
        grid_spec=grid_spec,
        out_shape=jax.ShapeDtypeStruct((n_seq, SAMPLE_ROWS, ATT_W), BF16),
        compiler_params=_params(1),
        name="fox_sample_attn",
    )(pt_flat, *seq_in, *([cache_k] * n_pages), *([cache_v] * n_pages), *([cache_lf] * n_pages))


def _rope_tables(pos, dim, rot_dim):
    half = rot_dim // 2
    inv = ROPE_THETA ** (-jnp.arange(half, dtype=F32) / half)
    ang = pos.astype(F32)[:, None] * inv[None, :]
    cos, sin = jnp.cos(ang), jnp.sin(ang)
    n = pos.shape[0]
    cos_t = jnp.concatenate([cos, cos, jnp.ones((n, dim - rot_dim), F32)], axis=1)
    sin_t = jnp.concatenate([-sin, sin, jnp.zeros((n, dim - rot_dim), F32)], axis=1)
    rep = LANES // dim
    return jnp.tile(cos_t, (1, rep)), jnp.tile(sin_t, (1, rep))


def kernel(x_prompt, x_sample, cache_dsa_k, cache_dsa_v, cache_dsa_kidx, cache_fox_k, cache_fox_v, cache_fox_logf,
           page_table, c_prompt, c_sample, norm_g, w_ada, b_ada, w_in_dsa, w_out_dsa, w_in_fox, b_forget,
           w_out_fox, final_g):
    n_b, t_p, d = x_prompt.shape
    n_s, t_s, _ = x_sample.shape
    depth = norm_g.shape[0]
    n_pages = page_table.shape[1]
    past = n_pages * PAGE_SIZE
    n_pool = cache_dsa_k.shape[1]
    assert t_s <= SAMPLE_ROWS // 2 and d == ATT_W

    n_c = n_b + n_s
    n_cp = -(-n_c // SUBLANES) * SUBLANES
    c_all = jnp.concatenate([c_prompt, c_sample, jnp.zeros((n_cp - n_c, d), F32)], axis=0)
    mod = _ada_call(c_all, w_ada, b_ada)

    pos_p = jnp.arange(t_p)
    pos_s = jnp.tile(past + jnp.arange(SAMPLE_ROWS), n_s)
    tabs_p = _rope_tables(pos_p, HEAD_DIM, ROT_DIM) + _rope_tables(pos_p, IDX_DIM, IDX_ROT_DIM)
    tabs_s = _rope_tables(pos_s, HEAD_DIM, ROT_DIM) + _rope_tables(pos_s, IDX_DIM, IDX_ROT_DIM)

    rows_p = n_b * t_p
    rows_s = n_s * SAMPLE_ROWS
    tm_p = _row_tile(t_p)
    tpb = t_p // tm_p
    tm_s = _row_tile(rows_s)
    xp = x_prompt.reshape(rows_p, d)
    xs = jnp.pad(x_sample, ((0, 0), (0, SAMPLE_ROWS - t_s), (0, 0))).reshape(rows_s, d)
    pt_flat = page_table.reshape(-1).astype(I32)

    ck_dsa = cache_dsa_k.reshape(cache_dsa_k.shape[0], n_pool, PAGE_SIZE, KV_W)
    cv_dsa = cache_dsa_v.reshape(cache_dsa_v.shape[0], n_pool, PAGE_SIZE, KV_W)
    ck_fox = cache_fox_k.reshape(cache_fox_k.shape[0], n_pool, PAGE_SIZE, KV_W)
    cv_fox = cache_fox_v.reshape(cache_fox_v.shape[0], n_pool, PAGE_SIZE, KV_W)

    fg = final_g.reshape(1, d)
    tq_fox = min(256, t_p)
    outs = {k: [] for k in ("dk_p", "dv_p", "dki_p", "fk_p", "fv_p", "fl_p", "dk_s", "dv_s", "dki_s", "fk_s", "fv_s", "fl_s")}

    def per_row(a):
        return jnp.repeat(a, SAMPLE_ROWS, axis=0).reshape(rows_s // tm_s, tm_s, d)

    for i in range(depth):
        j = i // N_MIXERS
        g = norm_g[i].reshape(1, d)
        shift_p, scale_p, gate_p = (mod[i, :n_b, k * d:(k + 1) * d].reshape(n_b, 1, d) for k in range(3))
        shift_s, scale_s, gate_s = (per_row(mod[i, n_b:n_c, k * d:(k + 1) * d]) for k in range(3))
        final = i == depth - 1
        if i % N_MIXERS == 0:
            w = w_in_dsa[j]
            n_main = 2 * ATT_W + 2 * KV_W + IDX_HEADS * IDX_DIM
            w_main = w[:, :n_main].astype(BF16)
            w_tail = jnp.pad(w[:, n_main:], ((0, 0), (0, LANES - (w.shape[1] - n_main)))).astype(BF16)
            w_out = w_out_dsa[j].astype(BF16)
            q, kf, kb, vf, vb, gz, qi, kif, kia, kib, wi = _inproj_dsa_call(
                xp, scale_p, shift_p, g, tabs_p, w_main, w_tail, tiles_per_batch=tpb, per_row=False)
            og = _dsa_prompt_call(q, qi, wi, gz, kb, vb, kia, kib, n_batch=n_b)
            xp = _outproj_call(og, xp, gate_p, w_out, fg, tiles_per_batch=tpb, per_row=False, final=final)
            outs["dk_p"].append(kf.reshape(n_b, t_p, N_KV_HEADS, HEAD_DIM))
            outs["dv_p"].append(vf.reshape(n_b, t_p, N_KV_HEADS, HEAD_DIM))
            outs["dki_p"].append(kif.reshape(n_b, t_p, IDX_DIM))
            q, kf, kb, vf, vb, gz, qi, kif, kia, kib, wi = _inproj_dsa_call(
                xs, scale_s, shift_s, g, tabs_s, w_main, w_tail, tiles_per_batch=1, per_row=True)
            r3 = lambda a: a.reshape(n_s, SAMPLE_ROWS, a.shape[-1])
            og = _dsa_sample_call(
                pt_flat, r3(q), qi.reshape(n_s, SAMPLE_ROWS * IDX_HEADS, IDX_DIM),
                wi.reshape(n_s, SAMPLE_ROWS * IDX_HEADS, 1), r3(gz), r3(kf), r3(vf), r3(kif),
                ck_dsa, cv_dsa, cache_dsa_kidx, layer=j, n_pages=n_pages, dec_seq=t_s)
            xs = _outproj_call(og.reshape(rows_s, ATT_W), xs, gate_s, w_out, fg, tiles_per_batch=1, per_row=True,
                               final=final)
            outs["dk_s"].append(r3(kf)[:, :t_s].reshape(n_s, t_s, N_KV_HEADS, HEAD_DIM))
            outs["dv_s"].append(r3(vf)[:, :t_s].reshape(n_s, t_s, N_KV_HEADS, HEAD_DIM))
            outs["dki_s"].append(r3(kif)[:, :t_s])
        else:
            w = w_in_fox[j]
            n_main = 2 * ATT_W + 2 * KV_W
            w_main = w[:, :n_main].astype(BF16)
            w_flt = jnp.pad(w[:, n_main:].T, ((0, 2 * SUBLANES - N_HEADS), (0, 0))).astype(BF16)
            b_f = b_forget[j].reshape(N_HEADS, 1)
            w_out = w_out_fox[j].astype(BF16)
            q, kf, kb, vf, vb, gz, lft = _inproj_fox_call(
                xp, scale_p, shift_p, g, w_main, w_flt, b_f, tiles_per_batch=tpb, per_row=False)
            c_t = _cumsum_call(lft, n_b)
            c_chunks = c_t.reshape(N_HEADS, n_b, t_p // tq_fox, tq_fox).transpose(1, 2, 0, 3)
            og = _fox_prompt_call(q, gz, kb, vb, c_chunks, n_batch=n_b)
            xp = _outproj_call(og, xp, gate_p, w_out, fg, tiles_per_batch=tpb, per_row=False, final=final)
            outs["fk_p"].append(kf.reshape(n_b, t_p, N_KV_HEADS, HEAD_DIM))
            outs["fv_p"].append(vf.reshape(n_b, t_p, N_KV_HEADS, HEAD_DIM))
            outs["fl_p"].append(lft.T.reshape(n_b, t_p, N_HEADS))
            q, kf, kb, vf, vb, gz, lft = _inproj_fox_call(
                xs, scale_s, shift_s, g, w_main, w_flt, b_f, tiles_per_batch=1, per_row=True)
            r3 = lambda a: a.reshape(n_s, SAMPLE_ROWS, a.shape[-1])
            lf3 = lft.reshape(N_HEADS, n_s, SAMPLE_ROWS).transpose(1, 0, 2)
            lfn = jnp.pad(lf3, ((0, 0), (0, 0), (0, PAGE_SIZE - SAMPLE_ROWS)))
            og = _fox_sample_call(pt_flat, r3(q), r3(gz), r3(kf), r3(vf), lfn, ck_fox, cv_fox, cache_fox_logf,
                                  layer=j, n_pages=n_pages)
            xs = _outproj_call(og.reshape(rows_s, ATT_W), xs, gate_s, w_out, fg, tiles_per_batch=1, per_row=True,
                               final=final)
            outs["fk_s"].append(r3(kf)[:, :t_s].reshape(n_s, t_s, N_KV_HEADS, HEAD_DIM))
            outs["fv_s"].append(r3(vf)[:, :t_s].reshape(n_s, t_s, N_KV_HEADS, HEAD_DIM))
            outs["fl_s"].append(lf3.transpose(0, 2, 1)[:, :t_s])

    y_prompt = xp.reshape(n_b, t_p, d)
    y_sample = xs.reshape(n_s, SAMPLE_ROWS, d)[:, :t_s]
    st = lambda k: jnp.stack(outs[k])
    return (y_prompt, y_sample, st("dk_p"), st("dv_p"), st("dki_p"), st("fk_p"), st("fv_p"), st("fl_p"),
            st("dk_s"), st("dv_s"), st("dki_s"), st("fk_s"), st("fv_s"), st("fl_s"))
```

```python
import functools
import math

import jax
import jax.numpy as jnp
import numpy as np
from jax import lax
from jax.experimental import pallas as pl
from jax.experimental.pallas import tpu as pltpu

N_HEADS = 8
HEAD_DIM = 128
N_KV_HEADS = 4
GROUP = N_HEADS // N_KV_HEADS
ROT_DIM = HEAD_DIM // 4
ROPE_THETA = 500000.0
IDX_HEADS = 8
IDX_DIM = 64
IDX_ROT_DIM = IDX_DIM // 4
TOPK_MAX = 256
PAGE_SIZE = 128
EPS = 1e-6
ATT_W = N_HEADS * HEAD_DIM
KV_W = N_KV_HEADS * HEAD_DIM
N_MIXERS = 2

LANES = 128
SUBLANES = 8
VMEM_LIMIT = 56 * 1024 * 1024

SAMPLE_ROWS = SUBLANES
NEG = -1e30
LOG2E = math.log2(math.e)
INT_MIN = -(2 ** 31)

F32 = jnp.float32
BF16 = jnp.bfloat16
I32 = jnp.int32


def _params(n_axes):
    return pltpu.CompilerParams(dimension_semantics=("arbitrary",) * n_axes, vmem_limit_bytes=VMEM_LIMIT)


def _dot(a, b):
    return jnp.dot(a, b, preferred_element_type=F32)


def _dot_nt(a, b):
    return lax.dot_general(a, b, (((1,), (1,)), ((), ())), preferred_element_type=F32)


def _sigmoid(x):
    return 1.0 / (1.0 + jnp.exp(-x))


def _ada_kernel(c_ref, w_ref, b_ref, o_ref):
    c = c_ref[...].astype(BF16)
    w = w_ref[0].astype(BF16)
    o_ref[0] = _dot(c, w) + b_ref[0]


def _ada_call(c_all, w_ada, b_ada):
    depth, d, n3 = w_ada.shape
    m = c_all.shape[0]
    tn = d
    return pl.pallas_call(
        _ada_kernel,
        grid=(depth, n3 // tn),
        in_specs=[
            pl.BlockSpec((m, d), lambda l, j: (0, 0)),
            pl.BlockSpec((1, d, tn), lambda l, j: (l, 0, j)),
            pl.BlockSpec((1, 1, tn), lambda l, j: (l, 0, j)),
        ],
        out_specs=pl.BlockSpec((1, m, tn), lambda l, j: (l, 0, j)),
        out_shape=jax.ShapeDtypeStruct((depth, m, n3), F32),
        compiler_params=_params(2),
        name="ada_mod",
    )(c_all, w_ada, b_ada.reshape(depth, 1, n3))


def _modulated_input(x_ref, sc_ref, sh_ref, g_ref):
    x = x_ref[...]
    ms = jnp.mean(x * x, axis=-1, keepdims=True)
    y = x * lax.rsqrt(ms + EPS) * g_ref[...]
    return (y * (1.0 + sc_ref[0]) + sh_ref[0]).astype(BF16)


def _rope(v, lane, cos_t, sin_t, period, half):
    vr = jnp.where((lane & (period - 1)) < half, pltpu.roll(v, LANES - half, 1), pltpu.roll(v, half, 1))
    return v * cos_t + vr * sin_t


def _inproj_dsa_kernel(x_ref, sc_ref, sh_ref, g_ref, c128_ref, s128_ref, c64_ref, s64_ref, w_ref, wt_ref,
                       q_ref, kf_ref, kb_ref, vf_ref, vb_ref, gz_ref, qi_ref, kif_ref, kia_ref, kib_ref, wi_ref,
                       *, q_scale, wi_scale):
    h = _modulated_input(x_ref, sc_ref, sh_ref, g_ref)
    tm = h.shape[0]
    lane = lax.broadcasted_iota(I32, (tm, LANES), 1)
    c128, s128, c64, s64 = c128_ref[...], s128_ref[...], c64_ref[...], s64_ref[...]
    rope_h = lambda v: _rope(v, lane, c128, s128, HEAD_DIM, ROT_DIM // 2)
    rope_i = lambda v: _rope(v, lane, c64, s64, IDX_DIM, IDX_ROT_DIM // 2)
    seg = 4 * LANES
    off = 0
    for s in range(ATT_W // seg):
        r = _dot(h, w_ref[:, off:off + seg])
        for j in range(seg // LANES):
            c0 = s * seg + j * LANES
            q_ref[:, c0:c0 + LANES] = (rope_h(r[:, j * LANES:(j + 1) * LANES]) * q_scale).astype(BF16)
        off += seg
    r = _dot(h, w_ref[:, off:off + KV_W])
    for j in range(N_KV_HEADS):
        kr = rope_h(r[:, j * LANES:(j + 1) * LANES])
        kf_ref[:, j * LANES:(j + 1) * LANES] = kr
        kb_ref[:, j * LANES:(j + 1) * LANES] = kr.astype(BF16)
    off += KV_W
    r = _dot(h, w_ref[:, off:off + KV_W])
    vf_ref[...] = r
    vb_ref[...] = r.astype(BF16)
    off += KV_W
    for s in range(ATT_W // seg):
        r = _dot(h, w_ref[:, off:off + seg])
        gz_ref[:, s * seg:(s + 1) * seg] = (r * _sigmoid(r)).astype(BF16)
        off += seg
    r = _dot(h, w_ref[:, off:off + IDX_HEADS * IDX_DIM])
    for j in range(IDX_HEADS * IDX_DIM // LANES):
        qi_ref[:, j * LANES:(j + 1) * LANES] = rope_i(r[:, j * LANES:(j + 1) * LANES]).astype(BF16)
    t = _dot(h, wt_ref[...])
    tr = rope_i(t)
    kif_ref[...] = tr[:, 0:IDX_DIM]
    ka = jnp.where(lane < IDX_DIM, tr, 0.0)
    kia_ref[...] = ka.astype(BF16)
    kib_ref[...] = pltpu.roll(ka, IDX_DIM, 1).astype(BF16)
    wi_ref[...] = t[:, IDX_DIM:IDX_DIM + IDX_HEADS] * wi_scale


def _inproj_fox_kernel(x_ref, sc_ref, sh_ref, g_ref, w_ref, wfl_ref, bf_ref,
                       q_ref, kf_ref, kb_ref, vf_ref, vb_ref, gz_ref, lft_ref, *, q_scale):
    h = _modulated_input(x_ref, sc_ref, sh_ref, g_ref)
    seg = 4 * LANES
    off = 0
    for s in range(ATT_W // seg):
        r = _dot(h, w_ref[:, off:off + seg])
        q_ref[:, s * seg:(s + 1) * seg] = (r * q_scale).astype(BF16)
        off += seg
    r = _dot(h, w_ref[:, off:off + KV_W])
    kf_ref[...] = r
    kb_ref[...] = r.astype(BF16)
    off += KV_W
    r = _dot(h, w_ref[:, off:off + KV_W])
    vf_ref[...] = r
    vb_ref[...] = r.astype(BF16)
    off += KV_W
    for s in range(ATT_W // seg):
        r = _dot(h, w_ref[:, off:off + seg])
        gz_ref[:, s * seg:(s + 1) * seg] = (r * _sigmoid(r)).astype(BF16)
        off += seg
    z = _dot_nt(wfl_ref[...], h)[0:N_HEADS, :] + bf_ref[...]
    lft_ref[...] = jnp.minimum(z, 0.0) - jnp.log1p(jnp.exp(-jnp.abs(z)))


def _row_tile(rows):
    return 512 if rows % 512 == 0 else rows


def _mod_specs(tm, d, tiles_per_batch, per_row):
    if per_row:
        return pl.BlockSpec((1, tm, d), lambda i: (i, 0, 0))
    return pl.BlockSpec((1, 1, d), lambda i: (i // tiles_per_batch, 0, 0))


def _inproj_dsa_call(x, scale, shift, g, tabs, w_main, w_tail, *, tiles_per_batch, per_row):
    rows, d = x.shape
    tm = _row_tile(rows) if per_row else rows // (scale.shape[0] * tiles_per_batch)
    n_t = tabs[0].shape[0] // tm
    row = lambda w: pl.BlockSpec((tm, w), lambda i: (i, 0))
    tab = pl.BlockSpec((tm, LANES), lambda i: (i % n_t, 0))
    mod = _mod_specs(tm, d, tiles_per_batch, per_row)
    full = lambda a: pl.BlockSpec(a.shape, lambda i: (0,) * a.ndim)
    outs = [(ATT_W, BF16), (KV_W, F32), (KV_W, BF16), (KV_W, F32), (KV_W, BF16), (ATT_W, BF16),
            (IDX_HEADS * IDX_DIM, BF16), (IDX_DIM, F32), (LANES, BF16), (LANES, BF16), (IDX_HEADS, F32)]
    return pl.pallas_call(
        functools.partial(_inproj_dsa_kernel, q_scale=HEAD_DIM ** -0.5 * LOG2E,
                          wi_scale=IDX_HEADS ** -0.5 * IDX_DIM ** -0.5),
        grid=(rows // tm,),
        in_specs=[row(d), mod, mod, full(g), tab, tab, tab, tab, full(w_main), full(w_tail)],
        out_specs=[row(w) for w, _ in outs],
        out_shape=[jax.ShapeDtypeStruct((rows, w), t) for w, t in outs],
        compiler_params=_params(1),
        name="inproj_dsa",
    )(x, scale, shift, g, *tabs, w_main, w_tail)


def _inproj_fox_call(x, scale, shift, g, w_main, w_flt, b_f, *, tiles_per_batch, per_row):
    rows, d = x.shape
    tm = _row_tile(rows) if per_row else rows // (scale.shape[0] * tiles_per_batch)
    row = lambda w: pl.BlockSpec((tm, w), lambda i: (i, 0))
    mod = _mod_specs(tm, d, tiles_per_batch, per_row)
    full = lambda a: pl.BlockSpec(a.shape, lambda i: (0,) * a.ndim)
    outs = [(ATT_W, BF16), (KV_W, F32), (KV_W, BF16), (KV_W, F32), (KV_W, BF16), (ATT_W, BF16)]
    return pl.pallas_call(
        functools.partial(_inproj_fox_kernel, q_scale=HEAD_DIM ** -0.5 * LOG2E),
        grid=(rows // tm,),
        in_specs=[row(d), mod, mod, full(g), full(w_main), full(w_flt), full(b_f)],
        out_specs=[row(w) for w, _ in outs] + [pl.BlockSpec((N_HEADS, tm), lambda i: (0, i))],
        out_shape=[jax.ShapeDtypeStruct((rows, w), t) for w, t in outs]
        + [jax.ShapeDtypeStruct((N_HEADS, rows), F32)],
        compiler_params=_params(1),
        name="inproj_fox",
    )(x, scale, shift, g, w_main, w_flt, b_f)


def _outproj_kernel(og_ref, x_ref, gate_ref, w_ref, fg_ref, o_ref, *, final):
    xn = x_ref[...] + gate_ref[0] * _dot(og_ref[...], w_ref[...])
    if final:
        ms = jnp.mean(xn * xn, axis=-1, keepdims=True)
        xn = xn * lax.rsqrt(ms + EPS) * fg_ref[...]
    o_ref[...] = xn


def _outproj_call(og, x, gate, w_out, final_g, *, tiles_per_batch, per_row, final):
    rows, d = x.shape
    tm = _row_tile(rows) if per_row else rows // (gate.shape[0] * tiles_per_batch)
    row = lambda w: pl.BlockSpec((tm, w), lambda i: (i, 0))
    full = lambda a: pl.BlockSpec(a.shape, lambda i: (0,) * a.ndim)
    return pl.pallas_call(
        functools.partial(_outproj_kernel, final=final),
        grid=(rows // tm,),
        in_specs=[row(ATT_W), row(d), _mod_specs(tm, d, tiles_per_batch, per_row), full(w_out), full(final_g)],
        out_specs=row(d),
        out_shape=jax.ShapeDtypeStruct((rows, d), F32),
        compiler_params=_params(1),
        name="outproj",
    )(og, x, gate, w_out, final_g)


def _lane_cumsum(x):
    n = x.shape[-1]
    lane = lax.broadcasted_iota(I32, x.shape, x.ndim - 1)
    s = 1
    while s < n:
        x = x + jnp.where(lane >= s, pltpu.roll(x, s, x.ndim - 1), 0.0)
        s *= 2
    return x


def _cumsum_kernel(x_ref, o_ref):
    o_ref[...] = _lane_cumsum(x_ref[...])


def _cumsum_call(lft, n_batch):
    h, rows = lft.shape
    t = rows // n_batch
    return pl.pallas_call(
        _cumsum_kernel,
        grid=(n_batch,),
        in_specs=[pl.BlockSpec((h, t), lambda b: (0, b))],
        out_specs=pl.BlockSpec((h, t), lambda b: (0, b)),
        out_shape=jax.ShapeDtypeStruct((h, rows), F32),
        compiler_params=_params(1),
        name="logf_cumsum",
    )(lft)


def _sort_key(score):
    bits = pltpu.bitcast(score, I32)
    key = bits ^ ((bits >> 31) & 0x7FFFFFFF)
    return jnp.where(score == 0.0, 0, key)


def _kth_largest(count_ge, k):
    kf = float(k)
    p0 = jnp.where(count_ge(jnp.int32(0)) >= kf, 0, INT_MIN).astype(I32)

    def body(it, prefix):
        cand = prefix + lax.shift_left(jnp.int32(1), 30 - it)
        return jnp.where(count_ge(cand) >= kf, cand, prefix)

    return lax.fori_loop(0, 31, body, p0)


def _tie_cut(count_tie_below, need, n_bits):
    def body(it, j):
        cand = j + lax.shift_left(jnp.int32(1), n_bits - 1 - it)
        return jnp.where(count_tie_below(cand) < need, cand, j)

    return lax.fori_loop(0, n_bits, body, jnp.zeros(need.shape, I32))


def _attn_init(q_ref, q2_scr, m_scr, l_scr, acc_scr):
    for kv in range(N_KV_HEADS):
        h0 = kv * GROUP
        q2_scr[kv] = jnp.concatenate(
            [q_ref[:, (h0 + g) * LANES:(h0 + g + 1) * LANES] for g in range(GROUP)], axis=0)
    m_scr[...] = jnp.full(m_scr.shape, NEG, F32)
    l_scr[...] = jnp.zeros(l_scr.shape, F32)
    acc_scr[...] = jnp.zeros(acc_scr.shape, F32)


def _online_softmax_step(s, vc, kv, m_scr, l_scr, acc_scr):
    m_old = m_scr[kv]
    mn = jnp.maximum(m_old, jnp.max(s, axis=1, keepdims=True))
    a = jnp.exp2(m_old - mn)
    p = jnp.exp2(s - mn)
    l_scr[kv] = a * l_scr[kv] + jnp.sum(p, axis=1, keepdims=True)
    acc_scr[kv] = a * acc_scr[kv] + _dot(p.astype(BF16), vc)
    m_scr[kv] = mn


def _attn_finish(o_ref, gz_ref, l_scr, acc_scr, tq):
    for kv in range(N_KV_HEADS):
        o = acc_scr[kv] / l_scr[kv]
        for g in range(GROUP):
            c0 = (kv * GROUP + g) * LANES
            o_ref[:, c0:c0 + LANES] = (o[g * tq:(g + 1) * tq] * gz_ref[:, c0:c0 + LANES].astype(F32)).astype(BF16)


def _attn_scratch(tq):
    rows = GROUP * tq
    return [pltpu.VMEM((N_KV_HEADS, rows, LANES), BF16), pltpu.VMEM((N_KV_HEADS, rows, 1), F32),
            pltpu.VMEM((N_KV_HEADS, rows, 1), F32), pltpu.VMEM((N_KV_HEADS, rows, LANES), F32)]


def _dsa_prompt_kernel(q_ref, qi_ref, wi_ref, gz_ref, k_ref, v_ref, kia_ref, kib_ref, o_ref, s_scr, b_scr,
                       q2_scr, m_scr, l_scr, acc_scr, *, tq, ck, topk, n_bits):
    i = pl.program_id(1)
    n_ck = ((i + 1) * tq + ck - 1) // ck
    row = i * tq + lax.broadcasted_iota(I32, (tq, ck), 0)
    lane = lax.broadcasted_iota(I32, (tq, ck), 1)
    wi = wi_ref[...]

    def score_chunk(c, carry):
        ka = kia_ref[pl.ds(c * ck, ck), :]
        kb = kib_ref[pl.ds(c * ck, ck), :]
        acc = None
        for p in range(IDX_HEADS // 2):
            slab = qi_ref[:, p * LANES:(p + 1) * LANES]
            for u, kk in enumerate((ka, kb)):
                hh = 2 * p + u
                term = jnp.maximum(_dot_nt(slab, kk), 0.0) * wi[:, hh:hh + 1]
                acc = term if acc is None else acc + term
        col = c * ck + lane
        s_scr[c] = jnp.where(col <= row, _sort_key(acc), INT_MIN)
        return carry

    lax.fori_loop(0, n_ck, score_chunk, 0)

    def count(pred):
        def body(c, acc):
            m = jnp.where(pred(s_scr[c], c * ck + lane), 1.0, 0.0)
            for u in range(ck // LANES):
                acc = acc + m[:, u * LANES:(u + 1) * LANES]
            return acc

        acc = lax.fori_loop(0, n_ck, body, jnp.zeros((tq, LANES), F32))
        return jnp.sum(acc, axis=1, keepdims=True)

    v = _kth_largest(lambda cand: count(lambda key, col: key >= cand), topk)
    n_gt = count(lambda key, col: key > v)
    n_ge = count(lambda key, col: key >= v)
    need = float(topk) - n_gt
    has_tie = jnp.max(jnp.where((n_ge > float(topk)) & (v != INT_MIN), 1.0, 0.0)) > 0.5
    jcut = lax.cond(
        has_tie,
        lambda: _tie_cut(lambda cand: count(lambda key, col: (key == v) & (col < cand)), need, n_bits),
        lambda: jnp.full((tq, 1), 2 ** n_bits, I32))

    def bias_chunk(c, carry):
        key = s_scr[c]
        col = c * ck + lane
        sel = ((key > v) | ((key == v) & (col <= jcut))) & (key != INT_MIN)
        b_scr[c] = jnp.where(sel, 0.0, NEG)
        return carry

    lax.fori_loop(0, n_ck, bias_chunk, 0)

    _attn_init(q_ref, q2_scr, m_scr, l_scr, acc_scr)

    def att_chunk(c, carry):
        b = b_scr[c]
        b2 = jnp.concatenate([b] * GROUP, axis=0)
        for kv in range(N_KV_HEADS):
            kc = k_ref[pl.ds(c * ck, ck), kv * LANES:(kv + 1) * LANES]
            vc = v_ref[pl.ds(c * ck, ck), kv * LANES:(kv + 1) * LANES]
            _online_softmax_step(_dot_nt(q2_scr[kv], kc) + b2, vc, kv, m_scr, l_scr, acc_scr)
        return carry

    lax.fori_loop(0, n_ck, att_chunk, 0)
    _attn_finish(o_ref, gz_ref, l_scr, acc_scr, tq)


def _dsa_prompt_call(q, qi, wi, gz, kb, vb, kia, kib, *, n_batch):
    rows = q.shape[0]
    t = rows // n_batch
    tq = min(128, t)
    ck = min(512, t)
    n_q = t // tq
    topk = min(TOPK_MAX, t // 4)
    n_bits = max(1, int(math.ceil(math.log2(t))))
    qrow = lambda w: pl.BlockSpec((tq, w), lambda b, i: (b * n_q + i, 0))
    seq = lambda w: pl.BlockSpec((t, w), lambda b, i: (b, 0))
    return pl.pallas_call(
        functools.partial(_dsa_prompt_kernel, tq=tq, ck=ck, topk=topk, n_bits=n_bits),
        grid=(n_batch, n_q),
        in_specs=[qrow(ATT_W), qrow(IDX_HEADS * IDX_DIM), qrow(IDX_HEADS), qrow(ATT_W),
                  seq(KV_W), seq(KV_W), seq(LANES), seq(LANES)],
        out_specs=qrow(ATT_W),
        out_shape=jax.ShapeDtypeStruct((rows, ATT_W), BF16),
        scratch_shapes=[pltpu.VMEM((t // ck, tq, ck), I32), pltpu.VMEM((t // ck, tq, ck), F32)] + _attn_scratch(tq),
        compiler_params=_params(2),
        name="dsa_prompt_attn",
    )(q, qi, wi, gz, kb, vb, kia, kib)


def _fox_prompt_kernel(q_ref, gz_ref, k_ref, v_ref, c_ref, o_ref, q2_scr, m_scr, l_scr, acc_scr, *, tq, ck):
    i = pl.program_id(1)
    n_ck = ((i + 1) * tq + ck - 1) // ck
    _attn_init(q_ref, q2_scr, m_scr, l_scr, acc_scr)

    def chunk(c, last):
        cj = c_ref[0, c] * LOG2E
        if last:
            row = i * tq + lax.broadcasted_iota(I32, (tq, ck), 0)
            col = c * ck + lax.broadcasted_iota(I32, (tq, ck), 1)
            causal = col <= row
        for kv in range(N_KV_HEADS):
            kc = k_ref[pl.ds(c * ck, ck), kv * LANES:(kv + 1) * LANES]
            vc = v_ref[pl.ds(c * ck, ck), kv * LANES:(kv + 1) * LANES]
            s = _dot_nt(q2_scr[kv], kc)
            parts = []
            for g in range(GROUP):
                h = kv * GROUP + g
                sg = s[g * tq:(g + 1) * tq] - cj[h:h + 1, :]
                parts.append(jnp.where(causal, sg, NEG) if last else sg)
            _online_softmax_step(jnp.concatenate(parts, axis=0), vc, kv, m_scr, l_scr, acc_scr)

    def body(c, carry):
        chunk(c, False)
        return carry

    lax.fori_loop(0, n_ck - 1, body, 0)
    chunk(n_ck - 1, True)
    _attn_finish(o_ref, gz_ref, l_scr, acc_scr, tq)


def _fox_prompt_call(q, gz, kb, vb, c_chunks, *, n_batch, tq):
    rows = q.shape[0]
    t = rows // n_batch
    n_c, ck = c_chunks.shape[1], c_chunks.shape[3]
    n_q = t // tq
    qrow = lambda w: pl.BlockSpec((tq, w), lambda b, i: (b * n_q + i, 0))
    seq = lambda w: pl.BlockSpec((t, w), lambda b, i: (b, 0))
    return pl.pallas_call(
        functools.partial(_fox_prompt_kernel, tq=tq, ck=ck),
        grid=(n_batch, n_q),
        in_specs=[qrow(ATT_W), qrow(ATT_W), seq(KV_W), seq(KV_W),
                  pl.BlockSpec((1, n_c, N_HEADS, ck), lambda b, i: (b, 0, 0, 0))],
        scratch_shapes=_attn_scratch(tq),
        out_specs=qrow(ATT_W),
        out_shape=jax.ShapeDtypeStruct((rows, ATT_W), BF16),
        compiler_params=_params(2),
        name="fox_prompt_attn",
    )(q, gz, kb, vb, c_chunks)


def _group_rows(x0, x1, row):
    return jnp.where(row < SAMPLE_ROWS // 2, x0, pltpu.roll(x1, SAMPLE_ROWS // 2, 0))


def _page_block(new_ref):
    new = new_ref[0]
    return jnp.concatenate([new, jnp.zeros((PAGE_SIZE - SAMPLE_ROWS, new.shape[1]), new.dtype)], axis=0)


def _kv_pages(page_refs, new_ref, kv):
    pages = [r[0, 0, pl.ds(kv, PAGE_SIZE, stride=N_KV_HEADS), :].astype(BF16) for r in page_refs]
    new = new_ref[0][:, kv * LANES:(kv + 1) * LANES]
    new = jnp.concatenate([new, jnp.zeros((PAGE_SIZE - SAMPLE_ROWS, LANES), new.dtype)], axis=0)
    return pages + [new.astype(BF16)]


def _sample_attend(q_ref, gz_ref, k_refs, kn_ref, v_refs, vn_ref, bias_fn, o_ref):
    row = lax.broadcasted_iota(I32, (SAMPLE_ROWS, LANES), 0)
    real = row < SAMPLE_ROWS // 2
    qf = q_ref[0].astype(F32)
    gz = gz_ref[0].astype(F32)
    for kv in range(N_KV_HEADS):
        h0 = kv * GROUP
        sl = lambda a, h: a[:, h * LANES:(h + 1) * LANES]
        q2 = _group_rows(sl(qf, h0), sl(qf, h0 + 1), row).astype(BF16)
        s = jnp.concatenate([_dot_nt(q2, kp) for kp in _kv_pages(k_refs, kn_ref, kv)], axis=1)
        s = s + bias_fn(kv)
        m = jnp.max(s, axis=1, keepdims=True)
        p = jnp.exp2(s - m)
        l = jnp.sum(p, axis=1, keepdims=True)
        pb = p.astype(BF16)
        o = None
        for n, vp in enumerate(_kv_pages(v_refs, vn_ref, kv)):
            t = _dot(pb[:, n * LANES:(n + 1) * LANES], vp)
            o = t if o is None else o + t
        o = o / l
        o_ref[0, :, h0 * LANES:(h0 + 1) * LANES] = jnp.where(real, o * sl(gz, h0), 0.0).astype(BF16)
        o_ref[0, :, (h0 + 1) * LANES:(h0 + 2) * LANES] = jnp.where(
            real, pltpu.roll(o, SAMPLE_ROWS // 2, 0) * sl(gz, h0 + 1), 0.0).astype(BF16)


def _dsa_sample_kernel(pt_ref, q_ref, qi_ref, wi_ref, gz_ref, kn_ref, vn_ref, kin_ref, *rest,
                       n_pages, topk, n_bits):
    k_refs, v_refs, ki_refs = rest[:n_pages], rest[n_pages:2 * n_pages], rest[2 * n_pages:3 * n_pages]
    o_ref = rest[3 * n_pages]
    n_chunks = n_pages + 1
    n_keys = n_chunks * PAGE_SIZE
    past = n_pages * PAGE_SIZE
    qi = qi_ref[0]
    wi = wi_ref[0]

    def score(dots):
        t = jnp.maximum(dots, 0.0) * wi
        return jnp.sum(t.reshape(SAMPLE_ROWS, IDX_HEADS, PAGE_SIZE), axis=1)

    chunks = [score(_dot(qi, r[0, 0].astype(BF16))) for r in ki_refs]
    chunks.append(score(_dot_nt(qi, _page_block(kin_ref).astype(BF16))))
    score_all = jnp.concatenate(chunks, axis=1)
    row = lax.broadcasted_iota(I32, (SAMPLE_ROWS, n_keys), 0)
    col = lax.broadcasted_iota(I32, (SAMPLE_ROWS, n_keys), 1)
    causal = col <= past + row
    key = jnp.where(causal, _sort_key(score_all), INT_MIN)

    count = lambda m: jnp.sum(jnp.where(m, 1.0, 0.0), axis=1, keepdims=True)
    v = _kth_largest(lambda cand: count(key >= cand), topk)
    n_gt = count(key > v)
    n_ge = count(key >= v)
    need = float(topk) - n_gt
    has_tie = jnp.max(jnp.where((n_ge > float(topk)) & (v != INT_MIN), 1.0, 0.0)) > 0.5
    jcut = lax.cond(
        has_tie,
        lambda: _tie_cut(lambda cand: count((key == v) & (col < cand)), need, n_bits),
        lambda: jnp.full((SAMPLE_ROWS, 1), 2 ** n_bits, I32))
    sel = ((key > v) | ((key == v) & (col <= jcut))) & causal
    bias = jnp.where(sel, 0.0, NEG)
    bias2 = jnp.where(row < SAMPLE_ROWS // 2, bias, pltpu.roll(bias, SAMPLE_ROWS // 2, 0))

    _sample_attend(q_ref, gz_ref, k_refs, kn_ref, v_refs, vn_ref, lambda kv: bias2, o_ref)


def _fox_sample_kernel(pt_ref, q_ref, gz_ref, kn_ref, vn_ref, lfn_ref, *rest, n_pages):
    k_refs, v_refs, lf_refs = rest[:n_pages], rest[n_pages:2 * n_pages], rest[2 * n_pages:3 * n_pages]
    o_ref = rest[3 * n_pages]
    n_keys = (n_pages + 1) * PAGE_SIZE
    past = n_pages * PAGE_SIZE
    lf_t = [r[0, 0] for r in lf_refs]
    c_all = _lane_cumsum(jnp.concatenate(lf_t + [lfn_ref[0]], axis=1)) * LOG2E
    row = lax.broadcasted_iota(I32, (SAMPLE_ROWS, n_keys), 0)
    col = lax.broadcasted_iota(I32, (SAMPLE_ROWS, n_keys), 1)
    tok = row & (SAMPLE_ROWS // 2 - 1)
    mask = jnp.where(col <= past + tok, 0.0, NEG)

    def bias_fn(kv):
        h0 = kv * GROUP
        c2 = jnp.where(row < SAMPLE_ROWS // 2, c_all[h0:h0 + 1, :], c_all[h0 + 1:h0 + 2, :])
        return mask - c2

    _sample_attend(q_ref, gz_ref, k_refs, kn_ref, v_refs, vn_ref, bias_fn, o_ref)


def _page_specs(layer, n_pages, rows, width):
    def spec(p):
        return pl.BlockSpec((1, 1, rows, width), lambda b, pt: (layer, pt[b * n_pages + p], 0, 0))
    return [spec(p) for p in range(n_pages)]


def _kv_page_specs(layer, n_pages):
    return _page_specs(layer, n_pages, PAGE_SIZE * N_KV_HEADS, HEAD_DIM)


def _seq_spec(a):
    return pl.BlockSpec((1,) + a.shape[1:], lambda b, pt: (b,) + (0,) * (a.ndim - 1))


def _dsa_sample_call(pt_flat, q, qi, wi, gz, kn, vn, kin, cache_k, cache_v, cache_ki, *, layer, n_pages, dec_seq):
    n_seq = q.shape[0]
    n_keys = (n_pages + 1) * PAGE_SIZE
    topk = min(TOPK_MAX, (n_pages * PAGE_SIZE + dec_seq) // 4)
    n_bits = int(math.ceil(math.log2(n_keys)))
    seq_in = [q, qi, wi, gz, kn, vn, kin]
    grid_spec = pltpu.PrefetchScalarGridSpec(
        num_scalar_prefetch=1,
        grid=(n_seq,),
        in_specs=[_seq_spec(a) for a in seq_in]
        + _kv_page_specs(layer, n_pages) + _kv_page_specs(layer, n_pages)
        + _page_specs(layer, n_pages, IDX_DIM, PAGE_SIZE),
        out_specs=pl.BlockSpec((1, SAMPLE_ROWS, ATT_W), lambda b, pt: (b, 0, 0)),
    )
    return pl.pallas_call(
        functools.partial(_dsa_sample_kernel, n_pages=n_pages, topk=topk, n_bits=n_bits),
        grid_spec=grid_spec,
        out_shape=jax.ShapeDtypeStruct((n_seq, SAMPLE_ROWS, ATT_W), BF16),
        compiler_params=_params(1),
        name="dsa_sample_attn",
    )(pt_flat, *seq_in, *([cache_k] * n_pages), *([cache_v] * n_pages), *([cache_ki] * n_pages))


def _fox_sample_call(pt_flat, q, gz, kn, vn, lfn, cache_k, cache_v, cache_lf, *, layer, n_pages):
    n_seq = q.shape[0]
    seq_in = [q, gz, kn, vn, lfn]
    grid_spec = pltpu.PrefetchScalarGridSpec(
        num_scalar_prefetch=1,
        grid=(n_seq,),
        in_specs=[_seq_spec(a) for a in seq_in]
        + _kv_page_specs(layer, n_pages) + _kv_page_specs(layer, n_pages)
        + _page_specs(layer, n_pages, N_HEADS, PAGE_SIZE),
        out_specs=pl.BlockSpec((1, SAMPLE_ROWS, ATT_W), lambda b, pt: (b, 0, 0)),
    )
    return pl.pallas_call(
        functools.partial(_fox_sample_kernel, n_pages=n_pages),
        grid_spec=grid_spec,
        out_shape=jax.ShapeDtypeStruct((n_seq, SAMPLE_ROWS, ATT_W), BF16),
        compiler_params=_params(1),
        name="fox_sample_attn",
    )(pt_flat, *seq_in, *([cache_k] * n_pages), *([cache_v] * n_pages), *([cache_lf] * n_pages))


def _rope_tables(pos, dim, rot_dim):
    half = rot_dim // 2
    inv = ROPE_THETA ** (-jnp.arange(half, dtype=F32) / half)
    ang = pos.astype(F32)[:, None] * inv[None, :]
    cos, sin = jnp.cos(ang), jnp.sin(ang)
    n = pos.shape[0]
    cos_t = jnp.concatenate([cos, cos, jnp.ones((n, dim - rot_dim), F32)], axis=1)
    sin_t = jnp.concatenate([-sin, sin, jnp.zeros((n, dim - rot_dim), F32)], axis=1)
    rep = LANES // dim
    return jnp.tile(cos_t, (1, rep)), jnp.tile(sin_t, (1, rep))


def kernel(x_prompt, x_sample, cache_dsa_k, cache_dsa_v, cache_dsa_kidx, cache_fox_k, cache_fox_v, cache_fox_logf,
           page_table, c_prompt, c_sample, norm_g, w_ada, b_ada, w_in_dsa, w_out_dsa, w_in_fox, b_forget,
           w_out_fox, final_g):
    n_b, t_p, d = x_prompt.shape
    n_s, t_s, _ = x_sample.shape
    depth = norm_g.shape[0]
    n_pages = page_table.shape[1]
    past = n_pages * PAGE_SIZE
    n_pool = cache_dsa_k.shape[1]
    assert t_s <= SAMPLE_ROWS // 2 and d == ATT_W

    n_c = n_b + n_s
    n_cp = -(-n_c // SUBLANES) * SUBLANES
    c_all = jnp.concatenate([c_prompt, c_sample, jnp.zeros((n_cp - n_c, d), F32)], axis=0)
    mod = _ada_call(c_all, w_ada, b_ada)

    pos_p = jnp.arange(t_p)
    pos_s = jnp.tile(past + jnp.arange(SAMPLE_ROWS), n_s)
    tabs_p = _rope_tables(pos_p, HEAD_DIM, ROT_DIM) + _rope_tables(pos_p, IDX_DIM, IDX_ROT_DIM)
    tabs_s = _rope_tables(pos_s, HEAD_DIM, ROT_DIM) + _rope_tables(pos_s, IDX_DIM, IDX_ROT_DIM)

    rows_p = n_b * t_p
    rows_s = n_s * SAMPLE_ROWS
    tm_p = _row_tile(t_p)
    tpb = t_p // tm_p
    tm_s = _row_tile(rows_s)
    xp = x_prompt.reshape(rows_p, d)
    xs = jnp.pad(x_sample, ((0, 0), (0, SAMPLE_ROWS - t_s), (0, 0))).reshape(rows_s, d)
    pt_flat = page_table.reshape(-1).astype(I32)

    page_view = lambda c: c.reshape(c.shape[0], n_pool, PAGE_SIZE * N_KV_HEADS, HEAD_DIM)
    ck_dsa, cv_dsa, ck_fox, cv_fox = (page_view(c) for c in (cache_dsa_k, cache_dsa_v, cache_fox_k, cache_fox_v))
    cki_dsa = jnp.swapaxes(cache_dsa_kidx, 2, 3)
    clf_fox = jnp.swapaxes(cache_fox_logf, 2, 3)

    fg = final_g.reshape(1, d)
    tq_fox = min(128, t_p)
    ck_fox_p = min(512, t_p)
    outs = {k: [] for k in ("dk_p", "dv_p", "dki_p", "fk_p", "fv_p", "fl_p", "dk_s", "dv_s", "dki_s", "fk_s", "fv_s", "fl_s")}

    def per_row(a):
        return jnp.repeat(a, SAMPLE_ROWS, axis=0).reshape(rows_s // tm_s, tm_s, d)

    for i in range(depth):
        j = i // N_MIXERS
        g = norm_g[i].reshape(1, d)
        shift_p, scale_p, gate_p = (mod[i, :n_b, k * d:(k + 1) * d].reshape(n_b, 1, d) for k in range(3))
        shift_s, scale_s, gate_s = (per_row(mod[i, n_b:n_c, k * d:(k + 1) * d]) for k in range(3))
        final = i == depth - 1
        if i % N_MIXERS == 0:
            w = w_in_dsa[j]
            n_main = 2 * ATT_W + 2 * KV_W + IDX_HEADS * IDX_DIM
            w_main = w[:, :n_main].astype(BF16)
            w_tail = jnp.pad(w[:, n_main:], ((0, 0), (0, LANES - (w.shape[1] - n_main)))).astype(BF16)
            w_out = w_out_dsa[j].astype(BF16)
            q, kf, kb, vf, vb, gz, qi, kif, kia, kib, wi = _inproj_dsa_call(
                xp, scale_p, shift_p, g, tabs_p, w_main, w_tail, tiles_per_batch=tpb, per_row=False)
            og = _dsa_prompt_call(q, qi, wi, gz, kb, vb, kia, kib, n_batch=n_b)
            xp = _outproj_call(og, xp, gate_p, w_out, fg, tiles_per_batch=tpb, per_row=False, final=final)
            outs["dk_p"].append(kf.reshape(n_b, t_p, N_KV_HEADS, HEAD_DIM))
            outs["dv_p"].append(vf.reshape(n_b, t_p, N_KV_HEADS, HEAD_DIM))
            outs["dki_p"].append(kif.reshape(n_b, t_p, IDX_DIM))
            q, kf, kb, vf, vb, gz, qi, kif, kia, kib, wi = _inproj_dsa_call(
                xs, scale_s, shift_s, g, tabs_s, w_main, w_tail, tiles_per_batch=1, per_row=True)
            r3 = lambda a: a.reshape(n_s, SAMPLE_ROWS, a.shape[-1])
            og = _dsa_sample_call(
                pt_flat, r3(q), qi.reshape(n_s, SAMPLE_ROWS * IDX_HEADS, IDX_DIM),
                wi.reshape(n_s, SAMPLE_ROWS * IDX_HEADS, 1), r3(gz), r3(kf), r3(vf), r3(kif),
                ck_dsa, cv_dsa, cki_dsa, layer=j, n_pages=n_pages, dec_seq=t_s)
            xs = _outproj_call(og.reshape(rows_s, ATT_W), xs, gate_s, w_out, fg, tiles_per_batch=1, per_row=True,
                               final=final)
            outs["dk_s"].append(r3(kf)[:, :t_s].reshape(n_s, t_s, N_KV_HEADS, HEAD_DIM))
            outs["dv_s"].append(r3(vf)[:, :t_s].reshape(n_s, t_s, N_KV_HEADS, HEAD_DIM))
            outs["dki_s"].append(r3(kif)[:, :t_s])
        else:
            w = w_in_fox[j]
            n_main = 2 * ATT_W + 2 * KV_W
            w_main = w[:, :n_main].astype(BF16)
            w_flt = jnp.pad(w[:, n_main:].T, ((0, 2 * SUBLANES - N_HEADS), (0, 0))).astype(BF16)
            b_f = b_forget[j].reshape(N_HEADS, 1)
            w_out = w_out_fox[j].astype(BF16)
            q, kf, kb, vf, vb, gz, lft = _inproj_fox_call(
                xp, scale_p, shift_p, g, w_main, w_flt, b_f, tiles_per_batch=tpb, per_row=False)
            c_t = _cumsum_call(lft, n_b)
            c_chunks = c_t.reshape(N_HEADS, n_b, t_p // ck_fox_p, ck_fox_p).transpose(1, 2, 0, 3)
            og = _fox_prompt_call(q, gz, kb, vb, c_chunks, n_batch=n_b, tq=tq_fox)
            xp = _outproj_call(og, xp, gate_p, w_out, fg, tiles_per_batch=tpb, per_row=False, final=final)
            outs["fk_p"].append(kf.reshape(n_b, t_p, N_KV_HEADS, HEAD_DIM))
            outs["fv_p"].append(vf.reshape(n_b, t_p, N_KV_HEADS, HEAD_DIM))
            outs["fl_p"].append(lft.T.reshape(n_b, t_p, N_HEADS))
            q, kf, kb, vf, vb, gz, lft = _inproj_fox_call(
                xs, scale_s, shift_s, g, w_main, w_flt, b_f, tiles_per_batch=1, per_row=True)
            r3 = lambda a: a.reshape(n_s, SAMPLE_ROWS, a.shape[-1])
            lf3 = lft.reshape(N_HEADS, n_s, SAMPLE_ROWS).transpose(1, 0, 2)
            lfn = jnp.pad(lf3, ((0, 0), (0, 0), (0, PAGE_SIZE - SAMPLE_ROWS)))
            og = _fox_sample_call(pt_flat, r3(q), r3(gz), r3(kf), r3(vf), lfn, ck_fox, cv_fox, clf_fox,
                                  layer=j, n_pages=n_pages)
            xs = _outproj_call(og.reshape(rows_s, ATT_W), xs, gate_s, w_out, fg, tiles_per_batch=1, per_row=True,
                               final=final)
            outs["fk_s"].append(r3(kf)[:, :t_s].reshape(n_s, t_s, N_KV_HEADS, HEAD_DIM))
            outs["fv_s"].append(r3(vf)[:, :t_s].reshape(n_s, t_s, N_KV_HEADS, HEAD_DIM))
            outs["fl_s"].append(lf3.transpose(0, 2, 1)[:, :t_s])

    y_prompt = xp.reshape(n_b, t_p, d)
    y_sample = xs.reshape(n_s, SAMPLE_ROWS, d)[:, :t_s]
    st = lambda k: jnp.stack(outs[k])
    return (y_prompt, y_sample, st("dk_p"), st("dv_p"), st("dki_p"), st("fk_p"), st("fv_p"), st("fl_p"),
            st("dk_s"), st("dv_s"), st("dki_s"), st("fk_s"), st("fv_s"), st("fl_s"))
```

```python
import functools
import math

import jax
import jax.numpy as jnp
import numpy as np
from jax import lax
from jax.experimental import pallas as pl
from jax.experimental.pallas import tpu as pltpu

N_HEADS = 8
HEAD_DIM = 128
N_KV_HEADS = 4
GROUP = N_HEADS // N_KV_HEADS
ROT_DIM = HEAD_DIM // 4
ROPE_THETA = 500000.0
IDX_HEADS = 8
IDX_DIM = 64
IDX_ROT_DIM = IDX_DIM // 4
TOPK_MAX = 256
PAGE_SIZE = 128
EPS = 1e-6
ATT_W = N_HEADS * HEAD_DIM
KV_W = N_KV_HEADS * HEAD_DIM
N_MIXERS = 2

LANES = 128
SUBLANES = 8
VMEM_LIMIT = 56 * 1024 * 1024

SAMPLE_ROWS = SUBLANES
NEG = -1e30
LOG2E = math.log2(math.e)
INT_MIN = -(2 ** 31)

F32 = jnp.float32
BF16 = jnp.bfloat16
I32 = jnp.int32


def _params(n_axes):
    return pltpu.CompilerParams(dimension_semantics=("arbitrary",) * n_axes, vmem_limit_bytes=VMEM_LIMIT)


def _dot(a, b):
    return jnp.dot(a, b, preferred_element_type=F32)


def _dot_nt(a, b):
    return lax.dot_general(a, b, (((1,), (1,)), ((), ())), preferred_element_type=F32)


def _sigmoid(x):
    return 1.0 / (1.0 + jnp.exp(-x))


def _ada_kernel(c_ref, w_ref, b_ref, o_ref):
    c = c_ref[...].astype(BF16)
    w = w_ref[0].astype(BF16)
    o_ref[0] = _dot(c, w) + b_ref[0]


def _ada_call(c_all, w_ada, b_ada):
    depth, d, n3 = w_ada.shape
    m = c_all.shape[0]
    tn = d
    return pl.pallas_call(
        _ada_kernel,
        grid=(depth, n3 // tn),
        in_specs=[
            pl.BlockSpec((m, d), lambda l, j: (0, 0)),
            pl.BlockSpec((1, d, tn), lambda l, j: (l, 0, j)),
            pl.BlockSpec((1, 1, tn), lambda l, j: (l, 0, j)),
        ],
        out_specs=pl.BlockSpec((1, m, tn), lambda l, j: (l, 0, j)),
        out_shape=jax.ShapeDtypeStruct((depth, m, n3), F32),
        compiler_params=_params(2),
        name="ada_mod",
    )(c_all, w_ada, b_ada.reshape(depth, 1, n3))


def _modulated_input(x_ref, sc_ref, sh_ref, g_ref):
    x = x_ref[...]
    ms = jnp.mean(x * x, axis=-1, keepdims=True)
    y = x * lax.rsqrt(ms + EPS) * g_ref[...]
    return (y * (1.0 + sc_ref[0]) + sh_ref[0]).astype(BF16)


def _rope(v, lane, cos_t, sin_t, period, half):
    vr = jnp.where((lane & (period - 1)) < half, pltpu.roll(v, LANES - half, 1), pltpu.roll(v, half, 1))
    return v * cos_t + vr * sin_t


def _inproj_dsa_kernel(x_ref, sc_ref, sh_ref, g_ref, c128_ref, s128_ref, c64_ref, s64_ref, w_ref, wt_ref,
                       wvt_ref, wwt_ref,
                       q_ref, kf_ref, kb_ref, vf_ref, vt_ref, gz_ref, qi_ref, kif_ref, kia_ref, kib_ref, wi_ref,
                       wit_ref, *, q_scale, wi_scale):
    h = _modulated_input(x_ref, sc_ref, sh_ref, g_ref)
    tm = h.shape[0]
    lane = lax.broadcasted_iota(I32, (tm, LANES), 1)
    c128, s128, c64, s64 = c128_ref[...], s128_ref[...], c64_ref[...], s64_ref[...]
    rope_h = lambda v: _rope(v, lane, c128, s128, HEAD_DIM, ROT_DIM // 2)
    rope_i = lambda v: _rope(v, lane, c64, s64, IDX_DIM, IDX_ROT_DIM // 2)
    seg = 4 * LANES
    off = 0
    for s in range(ATT_W // seg):
        r = _dot(h, w_ref[:, off:off + seg])
        for j in range(seg // LANES):
            c0 = s * seg + j * LANES
            q_ref[:, c0:c0 + LANES] = (rope_h(r[:, j * LANES:(j + 1) * LANES]) * q_scale).astype(BF16)
        off += seg
    r = _dot(h, w_ref[:, off:off + KV_W])
    for j in range(N_KV_HEADS):
        kr = rope_h(r[:, j * LANES:(j + 1) * LANES])
        kf_ref[:, j * LANES:(j + 1) * LANES] = kr
        kb_ref[:, j * LANES:(j + 1) * LANES] = kr.astype(BF16)
    off += KV_W
    r = _dot(h, w_ref[:, off:off + KV_W])
    vf_ref[...] = r
    vt_ref[0] = _dot_nt(wvt_ref[...], h).astype(BF16)
    off += KV_W
    for s in range(ATT_W // seg):
        r = _dot(h, w_ref[:, off:off + seg])
        gz_ref[:, s * seg:(s + 1) * seg] = (r * _sigmoid(r)).astype(BF16)
        off += seg
    r = _dot(h, w_ref[:, off:off + IDX_HEADS * IDX_DIM])
    for j in range(IDX_HEADS * IDX_DIM // LANES):
        qi_ref[:, j * LANES:(j + 1) * LANES] = rope_i(r[:, j * LANES:(j + 1) * LANES]).astype(BF16)
    t = _dot(h, wt_ref[...])
    tr = rope_i(t)
    kif_ref[...] = tr[:, 0:IDX_DIM]
    ka = jnp.where(lane < IDX_DIM, tr, 0.0)
    kia_ref[...] = ka.astype(BF16)
    kib_ref[...] = pltpu.roll(ka, IDX_DIM, 1).astype(BF16)
    wi_ref[...] = t[:, IDX_DIM:IDX_DIM + IDX_HEADS] * wi_scale
    wit_ref[...] = _dot_nt(wwt_ref[...], h)[0:IDX_HEADS, :] * wi_scale


def _inproj_fox_kernel(x_ref, sc_ref, sh_ref, g_ref, w_ref, wvt_ref, wfl_ref, bf_ref,
                       q_ref, kf_ref, kb_ref, vf_ref, vt_ref, gz_ref, lft_ref, *, q_scale):
    h = _modulated_input(x_ref, sc_ref, sh_ref, g_ref)
    seg = 4 * LANES
    off = 0
    for s in range(ATT_W // seg):
        r = _dot(h, w_ref[:, off:off + seg])
        q_ref[:, s * seg:(s + 1) * seg] = (r * q_scale).astype(BF16)
        off += seg
    r = _dot(h, w_ref[:, off:off + KV_W])
    kf_ref[...] = r
    kb_ref[...] = r.astype(BF16)
    off += KV_W
    r = _dot(h, w_ref[:, off:off + KV_W])
    vf_ref[...] = r
    vt_ref[0] = _dot_nt(wvt_ref[...], h).astype(BF16)
    off += KV_W
    for s in range(ATT_W // seg):
        r = _dot(h, w_ref[:, off:off + seg])
        gz_ref[:, s * seg:(s + 1) * seg] = (r * _sigmoid(r)).astype(BF16)
        off += seg
    z = _dot_nt(wfl_ref[...], h)[0:N_HEADS, :] + bf_ref[...]
    lft_ref[...] = jnp.minimum(z, 0.0) - jnp.log1p(jnp.exp(-jnp.abs(z)))


def _row_tile(rows):
    return 512 if rows % 512 == 0 else rows


def _mod_specs(tm, d, tiles_per_batch, per_row):
    if per_row:
        return pl.BlockSpec((1, tm, d), lambda i: (i, 0, 0))
    return pl.BlockSpec((1, 1, d), lambda i: (i // tiles_per_batch, 0, 0))


def _inproj_dsa_call(x, scale, shift, g, tabs, w_main, w_tail, w_vt, w_wit, *, tiles_per_batch, per_row):
    rows, d = x.shape
    tm = _row_tile(rows) if per_row else rows // (scale.shape[0] * tiles_per_batch)
    n_t = tabs[0].shape[0] // tm
    row = lambda w: pl.BlockSpec((tm, w), lambda i: (i, 0))
    tab = pl.BlockSpec((tm, LANES), lambda i: (i % n_t, 0))
    mod = _mod_specs(tm, d, tiles_per_batch, per_row)
    full = lambda a: pl.BlockSpec(a.shape, lambda i: (0,) * a.ndim)
    sd = jax.ShapeDtypeStruct
    outs = [(row(ATT_W), sd((rows, ATT_W), BF16)), (row(KV_W), sd((rows, KV_W), F32)),
            (row(KV_W), sd((rows, KV_W), BF16)), (row(KV_W), sd((rows, KV_W), F32)),
            (pl.BlockSpec((1, KV_W, tm), lambda i: (i, 0, 0)), sd((rows // tm, KV_W, tm), BF16)),
            (row(ATT_W), sd((rows, ATT_W), BF16)),
            (row(IDX_HEADS * IDX_DIM), sd((rows, IDX_HEADS * IDX_DIM), BF16)), (row(IDX_DIM), sd((rows, IDX_DIM), F32)),
            (row(LANES), sd((rows, LANES), BF16)), (row(LANES), sd((rows, LANES), BF16)),
            (row(IDX_HEADS), sd((rows, IDX_HEADS), F32)),
            (pl.BlockSpec((IDX_HEADS, tm), lambda i: (0, i)), sd((IDX_HEADS, rows), F32))]
    return pl.pallas_call(
        functools.partial(_inproj_dsa_kernel, q_scale=HEAD_DIM ** -0.5 * LOG2E,
                          wi_scale=IDX_HEADS ** -0.5 * IDX_DIM ** -0.5),
        grid=(rows // tm,),
        in_specs=[row(d), mod, mod, full(g), tab, tab, tab, tab, full(w_main), full(w_tail), full(w_vt), full(w_wit)],
        out_specs=[s for s, _ in outs],
        out_shape=[t for _, t in outs],
        compiler_params=_params(1),
        name="inproj_dsa",
    )(x, scale, shift, g, *tabs, w_main, w_tail, w_vt, w_wit)


def _inproj_fox_call(x, scale, shift, g, w_main, w_vt, w_flt, b_f, *, tiles_per_batch, per_row):
    rows, d = x.shape
    tm = _row_tile(rows) if per_row else rows // (scale.shape[0] * tiles_per_batch)
    row = lambda w: pl.BlockSpec((tm, w), lambda i: (i, 0))
    mod = _mod_specs(tm, d, tiles_per_batch, per_row)
    full = lambda a: pl.BlockSpec(a.shape, lambda i: (0,) * a.ndim)
    sd = jax.ShapeDtypeStruct
    return pl.pallas_call(
        functools.partial(_inproj_fox_kernel, q_scale=HEAD_DIM ** -0.5 * LOG2E),
        grid=(rows // tm,),
        in_specs=[row(d), mod, mod, full(g), full(w_main), full(w_vt), full(w_flt), full(b_f)],
        out_specs=[row(ATT_W), row(KV_W), row(KV_W), row(KV_W),
                   pl.BlockSpec((1, KV_W, tm), lambda i: (i, 0, 0)), row(ATT_W),
                   pl.BlockSpec((N_HEADS, tm), lambda i: (0, i))],
        out_shape=[sd((rows, ATT_W), BF16), sd((rows, KV_W), F32), sd((rows, KV_W), BF16), sd((rows, KV_W), F32),
                   sd((rows // tm, KV_W, tm), BF16), sd((rows, ATT_W), BF16), sd((N_HEADS, rows), F32)],
        compiler_params=_params(1),
        name="inproj_fox",
    )(x, scale, shift, g, w_main, w_vt, w_flt, b_f)


def _outproj_kernel(og_ref, x_ref, gate_ref, w_ref, fg_ref, o_ref, *, final):
    xn = x_ref[...] + gate_ref[0] * _dot(og_ref[...], w_ref[...])
    if final:
        ms = jnp.mean(xn * xn, axis=-1, keepdims=True)
        xn = xn * lax.rsqrt(ms + EPS) * fg_ref[...]
    o_ref[...] = xn


def _outproj_call(og, x, gate, w_out, final_g, *, tiles_per_batch, per_row, final):
    rows, d = x.shape
    tm = _row_tile(rows) if per_row else rows // (gate.shape[0] * tiles_per_batch)
    row = lambda w: pl.BlockSpec((tm, w), lambda i: (i, 0))
    full = lambda a: pl.BlockSpec(a.shape, lambda i: (0,) * a.ndim)
    return pl.pallas_call(
        functools.partial(_outproj_kernel, final=final),
        grid=(rows // tm,),
        in_specs=[row(ATT_W), row(d), _mod_specs(tm, d, tiles_per_batch, per_row), full(w_out), full(final_g)],
        out_specs=row(d),
        out_shape=jax.ShapeDtypeStruct((rows, d), F32),
        compiler_params=_params(1),
        name="outproj",
    )(og, x, gate, w_out, final_g)


def _lane_cumsum(x):
    n = x.shape[-1]
    lane = lax.broadcasted_iota(I32, x.shape, x.ndim - 1)
    s = 1
    while s < n:
        x = x + jnp.where(lane >= s, pltpu.roll(x, s, x.ndim - 1), 0.0)
        s *= 2
    return x


BIAS_PARTS = 3


def _fox_prep_kernel(lft_ref, kb_ref, kx_ref):
    t = lft_ref.shape[1]
    c = _lane_cumsum(lft_ref[...]) * (-LOG2E)
    ct = jnp.concatenate([c, jnp.zeros((LANES - N_HEADS, t), F32)], axis=0).T
    ext = jnp.zeros((t, LANES), F32)
    rest = ct
    for part in range(BIAS_PARTS):
        piece = rest.astype(BF16).astype(F32)
        rest = rest - piece
        ext = ext + (piece if part == 0 else pltpu.roll(piece, part * N_HEADS, 1))
    ext = ext.astype(BF16)
    for kv in range(N_KV_HEADS):
        kx_ref[:, 2 * kv * LANES:(2 * kv + 1) * LANES] = kb_ref[:, kv * LANES:(kv + 1) * LANES]
        kx_ref[:, (2 * kv + 1) * LANES:(2 * kv + 2) * LANES] = ext


def _fox_prep_call(lft, kb, n_batch):
    h, rows = lft.shape
    t = rows // n_batch
    return pl.pallas_call(
        _fox_prep_kernel,
        grid=(n_batch,),
        in_specs=[pl.BlockSpec((h, t), lambda b: (0, b)), pl.BlockSpec((t, KV_W), lambda b: (b, 0))],
        out_specs=pl.BlockSpec((t, 2 * KV_W), lambda b: (b, 0)),
        out_shape=jax.ShapeDtypeStruct((rows, 2 * KV_W), BF16),
        compiler_params=_params(1),
        name="fox_prep",
    )(lft, kb)


def _sort_key(score):
    bits = pltpu.bitcast(score, I32)
    key = bits ^ ((bits >> 31) & 0x7FFFFFFF)
    return jnp.where(score == 0.0, 0, key)


def _tree_sum(parts):
    while len(parts) > 1:
        parts = [parts[j] + parts[j + 1] for j in range(0, len(parts) - 1, 2)] + (parts[-1:] if len(parts) % 2 else [])
    return parts[0]


def _kth_largest(count_ge, k):
    kf = float(k)
    p0 = jnp.where(count_ge(jnp.int32(0)) >= kf, 0, INT_MIN).astype(I32)

    def body(it, prefix):
        cand = prefix + lax.shift_left(jnp.int32(1), 30 - it)
        return jnp.where(count_ge(cand) >= kf, cand, prefix)

    return lax.fori_loop(0, 31, body, p0)


def _tie_cut(count_tie_below, need, n_bits):
    def body(it, j):
        cand = j + lax.shift_left(jnp.int32(1), n_bits - 1 - it)
        return jnp.where(count_tie_below(cand) < need, cand, j)

    return lax.fori_loop(0, n_bits, body, jnp.zeros(need.shape, I32))


def _stage_major(n_items, stages):
    for stage in stages:
        for item in range(n_items):
            stage(item)


def _attn_scratch(tq, q_width):
    cols = GROUP * tq
    return [pltpu.VMEM((N_KV_HEADS, cols, q_width), BF16), pltpu.VMEM((N_KV_HEADS, 1, cols), F32),
            pltpu.VMEM((N_KV_HEADS, 1, cols), F32), pltpu.VMEM((N_KV_HEADS, LANES, cols), F32)]


def _attn_init(m_scr, l_scr, acc_scr):
    m_scr[...] = jnp.full(m_scr.shape, NEG, F32)
    l_scr[...] = jnp.zeros(l_scr.shape, F32)
    acc_scr[...] = jnp.zeros(acc_scr.shape, F32)


def _attn_chunk(logits_fn, vt_fn, m_scr, l_scr, acc_scr):
    s_all, p_all, a_all = {}, {}, {}

    def logits(kv):
        s_all[kv] = logits_fn(kv)

    def softmax(kv):
        s = s_all.pop(kv)
        m_old = m_scr[kv]
        mn = jnp.maximum(m_old, jnp.max(s, axis=0, keepdims=True))
        a_all[kv] = jnp.exp2(m_old - mn)
        p = jnp.exp2(s - mn)
        l_scr[kv] = a_all[kv] * l_scr[kv] + jnp.sum(p, axis=0, keepdims=True)
        m_scr[kv] = mn
        p_all[kv] = p.astype(BF16)

    def weighted_values(kv):
        acc_scr[kv] = a_all.pop(kv) * acc_scr[kv] + _dot(vt_fn(kv), p_all.pop(kv))

    _stage_major(N_KV_HEADS, (logits, softmax, weighted_values))


def _attn_finish(o_ref, gz_ref, l_scr, acc_scr, tq):
    for kv in range(N_KV_HEADS):
        o = (acc_scr[kv] / l_scr[kv]).T
        for g in range(GROUP):
            c0 = (kv * GROUP + g) * LANES
            o_ref[:, c0:c0 + LANES] = (o[g * tq:(g + 1) * tq] * gz_ref[:, c0:c0 + LANES].astype(F32)).astype(BF16)


def _dsa_prompt_kernel(q_ref, qi_ref, wit_ref, gz_ref, k_ref, vt_ref, kia_ref, kib_ref, o_ref, s_scr, b_scr,
                       q2_scr, m_scr, l_scr, acc_scr, *, tq, ck, topk, n_bits):
    i = pl.program_id(1)
    n_ck = ((i + 1) * tq + ck - 1) // ck
    kpos = lax.broadcasted_iota(I32, (ck, tq), 0)
    qpos = i * tq + lax.broadcasted_iota(I32, (ck, tq), 1)
    wit = wit_ref[...]

    def score_chunk(c, carry):
        ka = kia_ref[pl.ds(c * ck, ck), :]
        kb = kib_ref[pl.ds(c * ck, ck), :]
        acc = None
        for p in range(IDX_HEADS // 2):
            slab = qi_ref[:, p * LANES:(p + 1) * LANES]
            for u, kk in enumerate((ka, kb)):
                hh = 2 * p + u
                term = jnp.maximum(_dot_nt(kk, slab), 0.0) * wit[hh:hh + 1, :]
                acc = term if acc is None else acc + term
        s_scr[c] = jnp.where(c * ck + kpos <= qpos, _sort_key(acc), INT_MIN)
        return carry

    lax.fori_loop(0, n_ck, score_chunk, 0)

    def count(pred):
        def body(c, acc):
            m = jnp.where(pred(s_scr[c], c * ck + kpos), 1.0, 0.0)
            return acc + _tree_sum([m[r:r + SUBLANES] for r in range(0, ck, SUBLANES)])

        acc = lax.fori_loop(0, n_ck, body, jnp.zeros((SUBLANES, tq), F32))
        return jnp.sum(acc, axis=0, keepdims=True)

    v = _kth_largest(lambda cand: count(lambda key, pos: key >= cand), topk)
    n_gt = count(lambda key, pos: key > v)
    n_ge = count(lambda key, pos: key >= v)
    need = float(topk) - n_gt
    has_tie = jnp.max(jnp.where((n_ge > float(topk)) & (v != INT_MIN), 1.0, 0.0)) > 0.5
    jcut = lax.cond(
        has_tie,
        lambda: _tie_cut(lambda cand: count(lambda key, pos: (key == v) & (pos < cand)), need, n_bits),
        lambda: jnp.full((1, tq), 2 ** n_bits, I32))

    def bias_chunk(c, carry):
        key = s_scr[c]
        sel = ((key > v) | ((key == v) & (c * ck + kpos <= jcut))) & (key != INT_MIN)
        b_scr[c] = jnp.where(sel, 0.0, NEG)
        return carry

    lax.fori_loop(0, n_ck, bias_chunk, 0)

    for kv in range(N_KV_HEADS):
        for g in range(GROUP):
            h = kv * GROUP + g
            q2_scr[kv, g * tq:(g + 1) * tq, :] = q_ref[:, h * LANES:(h + 1) * LANES]
    _attn_init(m_scr, l_scr, acc_scr)

    def att_chunk(c, carry):
        b = b_scr[c]
        b2 = jnp.concatenate([b] * GROUP, axis=1)

        def logits_fn(kv):
            return _dot_nt(k_ref[pl.ds(c * ck, ck), kv * LANES:(kv + 1) * LANES], q2_scr[kv]) + b2

        _attn_chunk(logits_fn, lambda kv: vt_ref[c, kv * LANES:(kv + 1) * LANES, :], m_scr, l_scr, acc_scr)
        return carry

    lax.fori_loop(0, n_ck, att_chunk, 0)
    _attn_finish(o_ref, gz_ref, l_scr, acc_scr, tq)


def _dsa_prompt_call(q, qi, wit, gz, kb, vt, kia, kib, *, n_batch):
    rows = q.shape[0]
    t = rows // n_batch
    tq = min(128, t)
    ck = vt.shape[2]
    n_c = t // ck
    n_q = t // tq
    topk = min(TOPK_MAX, t // 4)
    n_bits = max(1, int(math.ceil(math.log2(t))))
    qrow = lambda w: pl.BlockSpec((tq, w), lambda b, i: (b * n_q + i, 0))
    seq = lambda w: pl.BlockSpec((t, w), lambda b, i: (b, 0))
    return pl.pallas_call(
        functools.partial(_dsa_prompt_kernel, tq=tq, ck=ck, topk=topk, n_bits=n_bits),
        grid=(n_batch, n_q),
        in_specs=[qrow(ATT_W), qrow(IDX_HEADS * IDX_DIM), pl.BlockSpec((IDX_HEADS, tq), lambda b, i: (0, b * n_q + i)),
                  qrow(ATT_W), seq(KV_W), pl.BlockSpec((n_c, KV_W, ck), lambda b, i: (b, 0, 0)),
                  seq(LANES), seq(LANES)],
        out_specs=qrow(ATT_W),
        out_shape=jax.ShapeDtypeStruct((rows, ATT_W), BF16),
        scratch_shapes=[pltpu.VMEM((n_c, ck, tq), I32), pltpu.VMEM((n_c, ck, tq), F32)] + _attn_scratch(tq, LANES),
        compiler_params=_params(2),
        name="dsa_prompt_attn",
    )(q, qi, wit, gz, kb, vt, kia, kib)


def _fox_prompt_kernel(q_ref, gz_ref, kx_ref, vt_ref, o_ref, q2_scr, m_scr, l_scr, acc_scr, *, tq, ck):
    i = pl.program_id(1)
    n_ck = ((i + 1) * tq + ck - 1) // ck
    lane = lax.broadcasted_iota(I32, (tq, LANES), 1)
    for kv in range(N_KV_HEADS):
        for g in range(GROUP):
            h = kv * GROUP + g
            ones = jnp.where(((lane & (N_HEADS - 1)) == h) & (lane < BIAS_PARTS * N_HEADS), 1.0, 0.0)
            q2_scr[kv, g * tq:(g + 1) * tq, 0:LANES] = q_ref[:, h * LANES:(h + 1) * LANES]
            q2_scr[kv, g * tq:(g + 1) * tq, LANES:2 * LANES] = ones.astype(BF16)
    _attn_init(m_scr, l_scr, acc_scr)

    def chunk(c, last):
        if last:
            key = c * ck + lax.broadcasted_iota(I32, (ck, GROUP * tq), 0)
            qpos = i * tq + (lax.broadcasted_iota(I32, (ck, GROUP * tq), 1) & (tq - 1))
            causal = key <= qpos

        def logits_fn(kv):
            kc = kx_ref[pl.ds(c * ck, ck), 2 * kv * LANES:(2 * kv + 2) * LANES]
            s = _dot_nt(kc, q2_scr[kv])
            return jnp.where(causal, s, NEG) if last else s

        _attn_chunk(logits_fn, lambda kv: vt_ref[c, kv * LANES:(kv + 1) * LANES, :], m_scr, l_scr, acc_scr)

    def body(c, carry):
        chunk(c, False)
        return carry

    lax.fori_loop(0, n_ck - 1, body, 0)
    chunk(n_ck - 1, True)
    _attn_finish(o_ref, gz_ref, l_scr, acc_scr, tq)


def _fox_prompt_call(q, gz, kx, vt, *, n_batch, tq):
    rows = q.shape[0]
    t = rows // n_batch
    ck = vt.shape[2]
    n_c = t // ck
    n_q = t // tq
    assert tq & (tq - 1) == 0 and BIAS_PARTS * N_HEADS <= LANES
    qrow = lambda w: pl.BlockSpec((tq, w), lambda b, i: (b * n_q + i, 0))
    return pl.pallas_call(
        functools.partial(_fox_prompt_kernel, tq=tq, ck=ck),
        grid=(n_batch, n_q),
        in_specs=[qrow(ATT_W), qrow(ATT_W), pl.BlockSpec((t, 2 * KV_W), lambda b, i: (b, 0)),
                  pl.BlockSpec((n_c, KV_W, ck), lambda b, i: (b, 0, 0))],
        scratch_shapes=_attn_scratch(tq, 2 * LANES),
        out_specs=qrow(ATT_W),
        out_shape=jax.ShapeDtypeStruct((rows, ATT_W), BF16),
        compiler_params=_params(2),
        name="fox_prompt_attn",
    )(q, gz, kx, vt)


def _group_rows(x0, x1, row):
    return jnp.where(row < SAMPLE_ROWS // 2, x0, pltpu.roll(x1, SAMPLE_ROWS // 2, 0))


def _page_block(new_ref):
    new = new_ref[0]
    return jnp.concatenate([new, jnp.zeros((PAGE_SIZE - SAMPLE_ROWS, new.shape[1]), new.dtype)], axis=0)


def _kv_pages(page_refs, new_ref, kv):
    pages = [r[0, 0, pl.ds(kv, PAGE_SIZE, stride=N_KV_HEADS), :].astype(BF16) for r in page_refs]
    new = new_ref[0][:, kv * LANES:(kv + 1) * LANES]
    new = jnp.concatenate([new, jnp.zeros((PAGE_SIZE - SAMPLE_ROWS, LANES), new.dtype)], axis=0)
    return pages + [new.astype(BF16)]


def _sample_attend(q_ref, gz_ref, k_refs, kn_ref, v_refs, vn_ref, bias_fn, o_ref):
    row = lax.broadcasted_iota(I32, (SAMPLE_ROWS, LANES), 0)
    real = row < SAMPLE_ROWS // 2
    qf = q_ref[0].astype(F32)
    gz = gz_ref[0].astype(F32)
    for kv in range(N_KV_HEADS):
        h0 = kv * GROUP
        sl = lambda a, h: a[:, h * LANES:(h + 1) * LANES]
        q2 = _group_rows(sl(qf, h0), sl(qf, h0 + 1), row).astype(BF16)
        s = jnp.concatenate([_dot_nt(q2, kp) for kp in _kv_pages(k_refs, kn_ref, kv)], axis=1)
        s = s + bias_fn(kv)
        m = jnp.max(s, axis=1, keepdims=True)
        p = jnp.exp2(s - m)
        l = jnp.sum(p, axis=1, keepdims=True)
        pb = p.astype(BF16)
        o = None
        for n, vp in enumerate(_kv_pages(v_refs, vn_ref, kv)):
            t = _dot(pb[:, n * LANES:(n + 1) * LANES], vp)
            o = t if o is None else o + t
        o = o / l
        o_ref[0, :, h0 * LANES:(h0 + 1) * LANES] = jnp.where(real, o * sl(gz, h0), 0.0).astype(BF16)
        o_ref[0, :, (h0 + 1) * LANES:(h0 + 2) * LANES] = jnp.where(
            real, pltpu.roll(o, SAMPLE_ROWS // 2, 0) * sl(gz, h0 + 1), 0.0).astype(BF16)


def _dsa_sample_kernel(pt_ref, q_ref, qi_ref, wi_ref, gz_ref, kn_ref, vn_ref, kin_ref, *rest,
                       n_pages, topk, n_bits):
    k_refs, v_refs, ki_refs = rest[:n_pages], rest[n_pages:2 * n_pages], rest[2 * n_pages:3 * n_pages]
    o_ref = rest[3 * n_pages]
    n_chunks = n_pages + 1
    n_keys = n_chunks * PAGE_SIZE
    past = n_pages * PAGE_SIZE
    qi = qi_ref[0]
    wi = wi_ref[0]

    def score(dots):
        t = jnp.maximum(dots, 0.0) * wi
        return jnp.sum(t.reshape(SAMPLE_ROWS, IDX_HEADS, PAGE_SIZE), axis=1)

    chunks = [score(_dot(qi, r[0, 0].astype(BF16))) for r in ki_refs]
    chunks.append(score(_dot_nt(qi, _page_block(kin_ref).astype(BF16))))
    score_all = jnp.concatenate(chunks, axis=1)
    row = lax.broadcasted_iota(I32, (SAMPLE_ROWS, n_keys), 0)
    col = lax.broadcasted_iota(I32, (SAMPLE_ROWS, n_keys), 1)
    causal = col <= past + row
    key = jnp.where(causal, _sort_key(score_all), INT_MIN)

    count = lambda m: jnp.sum(jnp.where(m, 1.0, 0.0), axis=1, keepdims=True)
    v = _kth_largest(lambda cand: count(key >= cand), topk)
    n_gt = count(key > v)
    n_ge = count(key >= v)
    need = float(topk) - n_gt
    has_tie = jnp.max(jnp.where((n_ge > float(topk)) & (v != INT_MIN), 1.0, 0.0)) > 0.5
    jcut = lax.cond(
        has_tie,
        lambda: _tie_cut(lambda cand: count((key == v) & (col < cand)), need, n_bits),
        lambda: jnp.full((SAMPLE_ROWS, 1), 2 ** n_bits, I32))
    sel = ((key > v) | ((key == v) & (col <= jcut))) & causal
    bias = jnp.where(sel, 0.0, NEG)
    bias2 = jnp.where(row < SAMPLE_ROWS // 2, bias, pltpu.roll(bias, SAMPLE_ROWS // 2, 0))

    _sample_attend(q_ref, gz_ref, k_refs, kn_ref, v_refs, vn_ref, lambda kv: bias2, o_ref)


def _fox_sample_kernel(pt_ref, q_ref, gz_ref, kn_ref, vn_ref, lfn_ref, *rest, n_pages):
    k_refs, v_refs, lf_refs = rest[:n_pages], rest[n_pages:2 * n_pages], rest[2 * n_pages:3 * n_pages]
    o_ref = rest[3 * n_pages]
    n_keys = (n_pages + 1) * PAGE_SIZE
    past = n_pages * PAGE_SIZE
    lf_t = [r[0, 0] for r in lf_refs]
    c_all = _lane_cumsum(jnp.concatenate(lf_t + [lfn_ref[0]], axis=1)) * LOG2E
    row = lax.broadcasted_iota(I32, (SAMPLE_ROWS, n_keys), 0)
    col = lax.broadcasted_iota(I32, (SAMPLE_ROWS, n_keys), 1)
    tok = row & (SAMPLE_ROWS // 2 - 1)
    mask = jnp.where(col <= past + tok, 0.0, NEG)

    def bias_fn(kv):
        h0 = kv * GROUP
        c2 = jnp.where(row < SAMPLE_ROWS // 2, c_all[h0:h0 + 1, :], c_all[h0 + 1:h0 + 2, :])
        return mask - c2

    _sample_attend(q_ref, gz_ref, k_refs, kn_ref, v_refs, vn_ref, bias_fn, o_ref)


def _page_specs(layer, n_pages, rows, width):
    def spec(p):
        return pl.BlockSpec((1, 1, rows, width), lambda b, pt: (layer, pt[b * n_pages + p], 0, 0))
    return [spec(p) for p in range(n_pages)]


def _kv_page_specs(layer, n_pages):
    return _page_specs(layer, n_pages, PAGE_SIZE * N_KV_HEADS, HEAD_DIM)


def _seq_spec(a):
    return pl.BlockSpec((1,) + a.shape[1:], lambda b, pt: (b,) + (0,) * (a.ndim - 1))


def _dsa_sample_call(pt_flat, q, qi, wi, gz, kn, vn, kin, cache_k, cache_v, cache_ki, *, layer, n_pages, dec_seq):
    n_seq = q.shape[0]
    n_keys = (n_pages + 1) * PAGE_SIZE
    topk = min(TOPK_MAX, (n_pages * PAGE_SIZE + dec_seq) // 4)
    n_bits = int(math.ceil(math.log2(n_keys)))
    seq_in = [q, qi, wi, gz, kn, vn, kin]
    grid_spec = pltpu.PrefetchScalarGridSpec(
        num_scalar_prefetch=1,
        grid=(n_seq,),
        in_specs=[_seq_spec(a) for a in seq_in]
        + _kv_page_specs(layer, n_pages) + _kv_page_specs(layer, n_pages)
        + _page_specs(layer, n_pages, IDX_DIM, PAGE_SIZE),
        out_specs=pl.BlockSpec((1, SAMPLE_ROWS, ATT_W), lambda b, pt: (b, 0, 0)),
    )
    return pl.pallas_call(
        functools.partial(_dsa_sample_kernel, n_pages=n_pages, topk=topk, n_bits=n_bits),
        grid_spec=grid_spec,
        out_shape=jax.ShapeDtypeStruct((n_seq, SAMPLE_ROWS, ATT_W), BF16),
        compiler_params=_params(1),
        name="dsa_sample_attn",
    )(pt_flat, *seq_in, *([cache_k] * n_pages), *([cache_v] * n_pages), *([cache_ki] * n_pages))


def _fox_sample_call(pt_flat, q, gz, kn, vn, lfn, cache_k, cache_v, cache_lf, *, layer, n_pages):
    n_seq = q.shape[0]
    seq_in = [q, gz, kn, vn, lfn]
    grid_spec = pltpu.PrefetchScalarGridSpec(
        num_scalar_prefetch=1,
        grid=(n_seq,),
        in_specs=[_seq_spec(a) for a in seq_in]
        + _kv_page_specs(layer, n_pages) + _kv_page_specs(layer, n_pages)
        + _page_specs(layer, n_pages, N_HEADS, PAGE_SIZE),
        out_specs=pl.BlockSpec((1, SAMPLE_ROWS, ATT_W), lambda b, pt: (b, 0, 0)),
    )
    return pl.pallas_call(
        functools.partial(_fox_sample_kernel, n_pages=n_pages),
        grid_spec=grid_spec,
        out_shape=jax.ShapeDtypeStruct((n_seq, SAMPLE_ROWS, ATT_W), BF16),
        compiler_params=_params(1),
        name="fox_sample_attn",
    )(pt_flat, *seq_in, *([cache_k] * n_pages), *([cache_v] * n_pages), *([cache_lf] * n_pages))


def _rope_tables(pos, dim, rot_dim):
    half = rot_dim // 2
    inv = ROPE_THETA ** (-jnp.arange(half, dtype=F32) / half)
    ang = pos.astype(F32)[:, None] * inv[None, :]
    cos, sin = jnp.cos(ang), jnp.sin(ang)
    n = pos.shape[0]
    cos_t = jnp.concatenate([cos, cos, jnp.ones((n, dim - rot_dim), F32)], axis=1)
    sin_t = jnp.concatenate([-sin, sin, jnp.zeros((n, dim - rot_dim), F32)], axis=1)
    rep = LANES // dim
    return jnp.tile(cos_t, (1, rep)), jnp.tile(sin_t, (1, rep))


def kernel(x_prompt, x_sample, cache_dsa_k, cache_dsa_v, cache_dsa_kidx, cache_fox_k, cache_fox_v, cache_fox_logf,
           page_table, c_prompt, c_sample, norm_g, w_ada, b_ada, w_in_dsa, w_out_dsa, w_in_fox, b_forget,
           w_out_fox, final_g):
    n_b, t_p, d = x_prompt.shape
    n_s, t_s, _ = x_sample.shape
    depth = norm_g.shape[0]
    n_pages = page_table.shape[1]
    past = n_pages * PAGE_SIZE
    n_pool = cache_dsa_k.shape[1]
    assert t_s <= SAMPLE_ROWS // 2 and d == ATT_W

    n_c = n_b + n_s
    n_cp = -(-n_c // SUBLANES) * SUBLANES
    c_all = jnp.concatenate([c_prompt, c_sample, jnp.zeros((n_cp - n_c, d), F32)], axis=0)
    mod = _ada_call(c_all, w_ada, b_ada)

    pos_p = jnp.arange(t_p)
    pos_s = jnp.tile(past + jnp.arange(SAMPLE_ROWS), n_s)
    tabs_p = _rope_tables(pos_p, HEAD_DIM, ROT_DIM) + _rope_tables(pos_p, IDX_DIM, IDX_ROT_DIM)
    tabs_s = _rope_tables(pos_s, HEAD_DIM, ROT_DIM) + _rope_tables(pos_s, IDX_DIM, IDX_ROT_DIM)

    rows_p = n_b * t_p
    rows_s = n_s * SAMPLE_ROWS
    tm_p = _row_tile(t_p)
    tpb = t_p // tm_p
    tm_s = _row_tile(rows_s)
    xp = x_prompt.reshape(rows_p, d)
    xs = jnp.pad(x_sample, ((0, 0), (0, SAMPLE_ROWS - t_s), (0, 0))).reshape(rows_s, d)
    pt_flat = page_table.reshape(-1).astype(I32)

    page_view = lambda c: c.reshape(c.shape[0], n_pool, PAGE_SIZE * N_KV_HEADS, HEAD_DIM)
    ck_dsa, cv_dsa, ck_fox, cv_fox = (page_view(c) for c in (cache_dsa_k, cache_dsa_v, cache_fox_k, cache_fox_v))
    cki_dsa = jnp.swapaxes(cache_dsa_kidx, 2, 3)
    clf_fox = jnp.swapaxes(cache_fox_logf, 2, 3)

    fg = final_g.reshape(1, d)
    tq_fox = min(128, t_p)
    outs = {k: [] for k in ("dk_p", "dv_p", "dki_p", "fk_p", "fv_p", "fl_p", "dk_s", "dv_s", "dki_s", "fk_s", "fv_s", "fl_s")}

    def per_row(a):
        return jnp.repeat(a, SAMPLE_ROWS, axis=0).reshape(rows_s // tm_s, tm_s, d)

    for i in range(depth):
        j = i // N_MIXERS
        g = norm_g[i].reshape(1, d)
        shift_p, scale_p, gate_p = (mod[i, :n_b, k * d:(k + 1) * d].reshape(n_b, 1, d) for k in range(3))
        shift_s, scale_s, gate_s = (per_row(mod[i, n_b:n_c, k * d:(k + 1) * d]) for k in range(3))
        final = i == depth - 1
        if i % N_MIXERS == 0:
            w = w_in_dsa[j]
            n_main = 2 * ATT_W + 2 * KV_W + IDX_HEADS * IDX_DIM
            w_main = w[:, :n_main].astype(BF16)
            w_tail = jnp.pad(w[:, n_main:], ((0, 0), (0, LANES - (w.shape[1] - n_main)))).astype(BF16)
            w_vt = w[:, ATT_W + KV_W:ATT_W + 2 * KV_W].T.astype(BF16)
            w_wit = jnp.pad(w[:, n_main + IDX_DIM:].T, ((0, 2 * SUBLANES - IDX_HEADS), (0, 0))).astype(BF16)
            w_out = w_out_dsa[j].astype(BF16)
            q, kf, kb, vf, vt, gz, qi, kif, kia, kib, wi, wit = _inproj_dsa_call(
                xp, scale_p, shift_p, g, tabs_p, w_main, w_tail, w_vt, w_wit, tiles_per_batch=tpb, per_row=False)
            og = _dsa_prompt_call(q, qi, wit, gz, kb, vt, kia, kib, n_batch=n_b)
            xp = _outproj_call(og, xp, gate_p, w_out, fg, tiles_per_batch=tpb, per_row=False, final=final)
            outs["dk_p"].append(kf.reshape(n_b, t_p, N_KV_HEADS, HEAD_DIM))
            outs["dv_p"].append(vf.reshape(n_b, t_p, N_KV_HEADS, HEAD_DIM))
            outs["dki_p"].append(kif.reshape(n_b, t_p, IDX_DIM))
            q, kf, kb, vf, vt, gz, qi, kif, kia, kib, wi, wit = _inproj_dsa_call(
                xs, scale_s, shift_s, g, tabs_s, w_main, w_tail, w_vt, w_wit, tiles_per_batch=1, per_row=True)
            r3 = lambda a: a.reshape(n_s, SAMPLE_ROWS, a.shape[-1])
            og = _dsa_sample_call(
                pt_flat, r3(q), qi.reshape(n_s, SAMPLE_ROWS * IDX_HEADS, IDX_DIM),
                wi.reshape(n_s, SAMPLE_ROWS * IDX_HEADS, 1), r3(gz), r3(kf), r3(vf), r3(kif),
                ck_dsa, cv_dsa, cki_dsa, layer=j, n_pages=n_pages, dec_seq=t_s)
            xs = _outproj_call(og.reshape(rows_s, ATT_W), xs, gate_s, w_out, fg, tiles_per_batch=1, per_row=True,
                               final=final)
            outs["dk_s"].append(r3(kf)[:, :t_s].reshape(n_s, t_s, N_KV_HEADS, HEAD_DIM))
            outs["dv_s"].append(r3(vf)[:, :t_s].reshape(n_s, t_s, N_KV_HEADS, HEAD_DIM))
            outs["dki_s"].append(r3(kif)[:, :t_s])
        else:
            w = w_in_fox[j]
            n_main = 2 * ATT_W + 2 * KV_W
            w_main = w[:, :n_main].astype(BF16)
            w_vt = w[:, ATT_W + KV_W:ATT_W + 2 * KV_W].T.astype(BF16)
            w_flt = jnp.pad(w[:, n_main:].T, ((0, 2 * SUBLANES - N_HEADS), (0, 0))).astype(BF16)
            b_f = b_forget[j].reshape(N_HEADS, 1)
            w_out = w_out_fox[j].astype(BF16)
            q, kf, kb, vf, vt, gz, lft = _inproj_fox_call(
                xp, scale_p, shift_p, g, w_main, w_vt, w_flt, b_f, tiles_per_batch=tpb, per_row=False)
            kx = _fox_prep_call(lft, kb, n_b)
            og = _fox_prompt_call(q, gz, kx, vt, n_batch=n_b, tq=tq_fox)
            xp = _outproj_call(og, xp, gate_p, w_out, fg, tiles_per_batch=tpb, per_row=False, final=final)
            outs["fk_p"].append(kf.reshape(n_b, t_p, N_KV_HEADS, HEAD_DIM))
            outs["fv_p"].append(vf.reshape(n_b, t_p, N_KV_HEADS, HEAD_DIM))
            outs["fl_p"].append(lft.T.reshape(n_b, t_p, N_HEADS))
            q, kf, kb, vf, vt, gz, lft = _inproj_fox_call(
                xs, scale_s, shift_s, g, w_main, w_vt, w_flt, b_f, tiles_per_batch=1, per_row=True)
            r3 = lambda a: a.reshape(n_s, SAMPLE_ROWS, a.shape[-1])
            lf3 = lft.reshape(N_HEADS, n_s, SAMPLE_ROWS).transpose(1, 0, 2)
            lfn = jnp.pad(lf3, ((0, 0), (0, 0), (0, PAGE_SIZE - SAMPLE_ROWS)))
            og = _fox_sample_call(pt_flat, r3(q), r3(gz), r3(kf), r3(vf), lfn, ck_fox, cv_fox, clf_fox,
                                  layer=j, n_pages=n_pages)
            xs = _outproj_call(og.reshape(rows_s, ATT_W), xs, gate_s, w_out, fg, tiles_per_batch=1, per_row=True,
                               final=final)
            outs["fk_s"].append(r3(kf)[:, :t_s].reshape(n_s, t_s, N_KV_HEADS, HEAD_DIM))
            outs["fv_s"].append(r3(vf)[:, :t_s].reshape(n_s, t_s, N_KV_HEADS, HEAD_DIM))
            outs["fl_s"].append(lf3.transpose(0, 2, 1)[:, :t_s])

    y_prompt = xp.reshape(n_b, t_p, d)
    y_sample = xs.reshape(n_s, SAMPLE_ROWS, d)[:, :t_s]
    st = lambda k: jnp.stack(outs[k])
    return (y_prompt, y_sample, st("dk_p"), st("dv_p"), st("dki_p"), st("fk_p"), st("fv_p"), st("fl_p"),
            st("dk_s"), st("dv_s"), st("dki_s"), st("fk_s"), st("fv_s"), st("fl_s"))
```

```python
import functools
import math

import jax
import jax.numpy as jnp
import numpy as np
from jax import lax
from jax.experimental import pallas as pl
from jax.experimental.pallas import tpu as pltpu

N_HEADS = 8
HEAD_DIM = 128
N_KV_HEADS = 4
GROUP = N_HEADS // N_KV_HEADS
ROT_DIM = HEAD_DIM // 4
ROPE_THETA = 500000.0
IDX_HEADS = 8
IDX_DIM = 64
IDX_ROT_DIM = IDX_DIM // 4
TOPK_MAX = 256
PAGE_SIZE = 128
EPS = 1e-6
ATT_W = N_HEADS * HEAD_DIM
KV_W = N_KV_HEADS * HEAD_DIM
N_MIXERS = 2

LANES = 128
SUBLANES = 8
VMEM_LIMIT = 56 * 1024 * 1024

SAMPLE_ROWS = SUBLANES
NEG = -1e30
LOG2E = math.log2(math.e)
INT_MIN = -(2 ** 31)

F32 = jnp.float32
BF16 = jnp.bfloat16
I32 = jnp.int32


def _params(n_axes):
    return pltpu.CompilerParams(dimension_semantics=("arbitrary",) * n_axes, vmem_limit_bytes=VMEM_LIMIT)


def _dot(a, b):
    return jnp.dot(a, b, preferred_element_type=F32)


def _dot_nt(a, b):
    return lax.dot_general(a, b, (((1,), (1,)), ((), ())), preferred_element_type=F32)


def _sigmoid(x):
    return 1.0 / (1.0 + jnp.exp(-x))


def _ada_kernel(c_ref, w_ref, b_ref, o_ref):
    c = c_ref[...].astype(BF16)
    w = w_ref[0].astype(BF16)
    o_ref[0] = _dot(c, w) + b_ref[0]


def _ada_call(c_all, w_ada, b_ada):
    depth, d, n3 = w_ada.shape
    m = c_all.shape[0]
    tn = d
    return pl.pallas_call(
        _ada_kernel,
        grid=(depth, n3 // tn),
        in_specs=[
            pl.BlockSpec((m, d), lambda l, j: (0, 0)),
            pl.BlockSpec((1, d, tn), lambda l, j: (l, 0, j)),
            pl.BlockSpec((1, 1, tn), lambda l, j: (l, 0, j)),
        ],
        out_specs=pl.BlockSpec((1, m, tn), lambda l, j: (l, 0, j)),
        out_shape=jax.ShapeDtypeStruct((depth, m, n3), F32),
        compiler_params=_params(2),
        name="ada_mod",
    )(c_all, w_ada, b_ada.reshape(depth, 1, n3))


def _modulated_input(x_ref, sc_ref, sh_ref, g_ref):
    x = x_ref[...]
    ms = jnp.mean(x * x, axis=-1, keepdims=True)
    y = x * lax.rsqrt(ms + EPS) * g_ref[...]
    return (y * (1.0 + sc_ref[0]) + sh_ref[0]).astype(BF16)


def _store_head(ref, head, value):
    ref[pl.ds(head, value.shape[0], stride=N_KV_HEADS), :] = value


def _head_rows_spec(tm):
    return pl.BlockSpec((tm * N_KV_HEADS, HEAD_DIM), lambda i: (i, 0))


def _rope(v, lane, cos_t, sin_t, period, half):
    vr = jnp.where((lane & (period - 1)) < half, pltpu.roll(v, LANES - half, 1), pltpu.roll(v, half, 1))
    return v * cos_t + vr * sin_t


def _inproj_dsa_kernel(x_ref, sc_ref, sh_ref, g_ref, c128_ref, s128_ref, c64_ref, s64_ref, w_ref, wt_ref,
                       wvt_ref, wwt_ref,
                       q_ref, kf_ref, kb_ref, vf_ref, vt_ref, gz_ref, qi_ref, kif_ref, kia_ref, kib_ref, wi_ref,
                       wit_ref, *, q_scale, wi_scale):
    h = _modulated_input(x_ref, sc_ref, sh_ref, g_ref)
    tm = h.shape[0]
    lane = lax.broadcasted_iota(I32, (tm, LANES), 1)
    c128, s128, c64, s64 = c128_ref[...], s128_ref[...], c64_ref[...], s64_ref[...]
    rope_h = lambda v: _rope(v, lane, c128, s128, HEAD_DIM, ROT_DIM // 2)
    rope_i = lambda v: _rope(v, lane, c64, s64, IDX_DIM, IDX_ROT_DIM // 2)
    seg = 4 * LANES
    off = 0
    for s in range(ATT_W // seg):
        r = _dot(h, w_ref[:, off:off + seg])
        for j in range(seg // LANES):
            c0 = s * seg + j * LANES
            q_ref[:, c0:c0 + LANES] = (rope_h(r[:, j * LANES:(j + 1) * LANES]) * q_scale).astype(BF16)
        off += seg
    r = _dot(h, w_ref[:, off:off + KV_W])
    for j in range(N_KV_HEADS):
        kr = rope_h(r[:, j * LANES:(j + 1) * LANES])
        _store_head(kf_ref, j, kr)
        kb_ref[:, j * LANES:(j + 1) * LANES] = kr.astype(BF16)
    off += KV_W
    r = _dot(h, w_ref[:, off:off + KV_W])
    for j in range(N_KV_HEADS):
        _store_head(vf_ref, j, r[:, j * LANES:(j + 1) * LANES])
    vt_ref[0] = _dot_nt(wvt_ref[...], h).astype(BF16)
    off += KV_W
    for s in range(ATT_W // seg):
        r = _dot(h, w_ref[:, off:off + seg])
        gz_ref[:, s * seg:(s + 1) * seg] = (r * _sigmoid(r)).astype(BF16)
        off += seg
    r = _dot(h, w_ref[:, off:off + IDX_HEADS * IDX_DIM])
    for j in range(IDX_HEADS * IDX_DIM // LANES):
        qi_ref[:, j * LANES:(j + 1) * LANES] = rope_i(r[:, j * LANES:(j + 1) * LANES]).astype(BF16)
    t = _dot(h, wt_ref[...])
    tr = rope_i(t)
    kif_ref[...] = tr[:, 0:IDX_DIM]
    ka = jnp.where(lane < IDX_DIM, tr, 0.0)
    kia_ref[...] = ka.astype(BF16)
    kib_ref[...] = pltpu.roll(ka, IDX_DIM, 1).astype(BF16)
    wi_ref[...] = t[:, IDX_DIM:IDX_DIM + IDX_HEADS] * wi_scale
    wit_ref[...] = _dot_nt(wwt_ref[...], h)[0:IDX_HEADS, :] * wi_scale


def _inproj_fox_kernel(x_ref, sc_ref, sh_ref, g_ref, w_ref, wvt_ref, wfl_ref, bf_ref,
                       q_ref, kf_ref, kb_ref, vf_ref, vt_ref, gz_ref, lft_ref, *, q_scale):
    h = _modulated_input(x_ref, sc_ref, sh_ref, g_ref)
    seg = 4 * LANES
    off = 0
    for s in range(ATT_W // seg):
        r = _dot(h, w_ref[:, off:off + seg])
        q_ref[:, s * seg:(s + 1) * seg] = (r * q_scale).astype(BF16)
        off += seg
    r = _dot(h, w_ref[:, off:off + KV_W])
    for j in range(N_KV_HEADS):
        _store_head(kf_ref, j, r[:, j * LANES:(j + 1) * LANES])
    kb_ref[...] = r.astype(BF16)
    off += KV_W
    r = _dot(h, w_ref[:, off:off + KV_W])
    for j in range(N_KV_HEADS):
        _store_head(vf_ref, j, r[:, j * LANES:(j + 1) * LANES])
    vt_ref[0] = _dot_nt(wvt_ref[...], h).astype(BF16)
    off += KV_W
    for s in range(ATT_W // seg):
        r = _dot(h, w_ref[:, off:off + seg])
        gz_ref[:, s * seg:(s + 1) * seg] = (r * _sigmoid(r)).astype(BF16)
        off += seg
    z = _dot_nt(wfl_ref[...], h)[0:N_HEADS, :] + bf_ref[...]
    lft_ref[...] = jnp.minimum(z, 0.0) - jnp.log1p(jnp.exp(-jnp.abs(z)))


def _row_tile(rows):
    return 512 if rows % 512 == 0 else rows


def _mod_specs(tm, d, tiles_per_batch, per_row):
    if per_row:
        return pl.BlockSpec((1, tm, d), lambda i: (i, 0, 0))
    return pl.BlockSpec((1, 1, d), lambda i: (i // tiles_per_batch, 0, 0))


def _inproj_dsa_call(x, scale, shift, g, tabs, w_main, w_tail, w_vt, w_wit, *, tiles_per_batch, per_row):
    rows, d = x.shape
    tm = _row_tile(rows) if per_row else rows // (scale.shape[0] * tiles_per_batch)
    n_t = tabs[0].shape[0] // tm
    row = lambda w: pl.BlockSpec((tm, w), lambda i: (i, 0))
    tab = pl.BlockSpec((tm, LANES), lambda i: (i % n_t, 0))
    mod = _mod_specs(tm, d, tiles_per_batch, per_row)
    full = lambda a: pl.BlockSpec(a.shape, lambda i: (0,) * a.ndim)
    sd = jax.ShapeDtypeStruct
    heads_f32 = (_head_rows_spec(tm), sd((rows * N_KV_HEADS, HEAD_DIM), F32))
    outs = [(row(ATT_W), sd((rows, ATT_W), BF16)), heads_f32, (row(KV_W), sd((rows, KV_W), BF16)), heads_f32,
            (pl.BlockSpec((1, KV_W, tm), lambda i: (i, 0, 0)), sd((rows // tm, KV_W, tm), BF16)),
            (row(ATT_W), sd((rows, ATT_W), BF16)),
            (row(IDX_HEADS * IDX_DIM), sd((rows, IDX_HEADS * IDX_DIM), BF16)), (row(IDX_DIM), sd((rows, IDX_DIM), F32)),
            (row(LANES), sd((rows, LANES), BF16)), (row(LANES), sd((rows, LANES), BF16)),
            (row(IDX_HEADS), sd((rows, IDX_HEADS), F32)),
            (pl.BlockSpec((IDX_HEADS, tm), lambda i: (0, i)), sd((IDX_HEADS, rows), F32))]
    return pl.pallas_call(
        functools.partial(_inproj_dsa_kernel, q_scale=HEAD_DIM ** -0.5 * LOG2E,
                          wi_scale=IDX_HEADS ** -0.5 * IDX_DIM ** -0.5),
        grid=(rows // tm,),
        in_specs=[row(d), mod, mod, full(g), tab, tab, tab, tab, full(w_main), full(w_tail), full(w_vt), full(w_wit)],
        out_specs=[s for s, _ in outs],
        out_shape=[t for _, t in outs],
        compiler_params=_params(1),
        name="inproj_dsa",
    )(x, scale, shift, g, *tabs, w_main, w_tail, w_vt, w_wit)


def _inproj_fox_call(x, scale, shift, g, w_main, w_vt, w_flt, b_f, *, tiles_per_batch, per_row):
    rows, d = x.shape
    tm = _row_tile(rows) if per_row else rows // (scale.shape[0] * tiles_per_batch)
    row = lambda w: pl.BlockSpec((tm, w), lambda i: (i, 0))
    mod = _mod_specs(tm, d, tiles_per_batch, per_row)
    full = lambda a: pl.BlockSpec(a.shape, lambda i: (0,) * a.ndim)
    sd = jax.ShapeDtypeStruct
    return pl.pallas_call(
        functools.partial(_inproj_fox_kernel, q_scale=HEAD_DIM ** -0.5 * LOG2E),
        grid=(rows // tm,),
        in_specs=[row(d), mod, mod, full(g), full(w_main), full(w_vt), full(w_flt), full(b_f)],
        out_specs=[row(ATT_W), _head_rows_spec(tm), row(KV_W), _head_rows_spec(tm),
                   pl.BlockSpec((1, KV_W, tm), lambda i: (i, 0, 0)), row(ATT_W),
                   pl.BlockSpec((N_HEADS, tm), lambda i: (0, i))],
        out_shape=[sd((rows, ATT_W), BF16), sd((rows * N_KV_HEADS, HEAD_DIM), F32), sd((rows, KV_W), BF16),
                   sd((rows * N_KV_HEADS, HEAD_DIM), F32),
                   sd((rows // tm, KV_W, tm), BF16), sd((rows, ATT_W), BF16), sd((N_HEADS, rows), F32)],
        compiler_params=_params(1),
        name="inproj_fox",
    )(x, scale, shift, g, w_main, w_vt, w_flt, b_f)


def _outproj_kernel(og_ref, x_ref, gate_ref, w_ref, fg_ref, o_ref, *, final):
    xn = x_ref[...] + gate_ref[0] * _dot(og_ref[...], w_ref[...])
    if final:
        ms = jnp.mean(xn * xn, axis=-1, keepdims=True)
        xn = xn * lax.rsqrt(ms + EPS) * fg_ref[...]
    o_ref[...] = xn


def _outproj_call(og, x, gate, w_out, final_g, *, tiles_per_batch, per_row, final):
    rows, d = x.shape
    tm = _row_tile(rows) if per_row else rows // (gate.shape[0] * tiles_per_batch)
    row = lambda w: pl.BlockSpec((tm, w), lambda i: (i, 0))
    full = lambda a: pl.BlockSpec(a.shape, lambda i: (0,) * a.ndim)
    return pl.pallas_call(
        functools.partial(_outproj_kernel, final=final),
        grid=(rows // tm,),
        in_specs=[row(ATT_W), row(d), _mod_specs(tm, d, tiles_per_batch, per_row), full(w_out), full(final_g)],
        out_specs=row(d),
        out_shape=jax.ShapeDtypeStruct((rows, d), F32),
        compiler_params=_params(1),
        name="outproj",
    )(og, x, gate, w_out, final_g)


def _lane_cumsum(x):
    n = x.shape[-1]
    lane = lax.broadcasted_iota(I32, x.shape, x.ndim - 1)
    s = 1
    while s < n:
        x = x + jnp.where(lane >= s, pltpu.roll(x, s, x.ndim - 1), 0.0)
        s *= 2
    return x


BIAS_PARTS = 3


def _fox_prep_kernel(lft_ref, kb_ref, kx_ref):
    t = lft_ref.shape[1]
    c = _lane_cumsum(lft_ref[...]) * (-LOG2E)
    ct = jnp.concatenate([c, jnp.zeros((LANES - N_HEADS, t), F32)], axis=0).T
    ext = jnp.zeros((t, LANES), F32)
    rest = ct
    for part in range(BIAS_PARTS):
        piece = rest.astype(BF16).astype(F32)
        rest = rest - piece
        ext = ext + (piece if part == 0 else pltpu.roll(piece, part * N_HEADS, 1))
    ext = ext.astype(BF16)
    for kv in range(N_KV_HEADS):
        kx_ref[:, 2 * kv * LANES:(2 * kv + 1) * LANES] = kb_ref[:, kv * LANES:(kv + 1) * LANES]
        kx_ref[:, (2 * kv + 1) * LANES:(2 * kv + 2) * LANES] = ext


def _fox_prep_call(lft, kb, n_batch):
    h, rows = lft.shape
    t = rows // n_batch
    return pl.pallas_call(
        _fox_prep_kernel,
        grid=(n_batch,),
        in_specs=[pl.BlockSpec((h, t), lambda b: (0, b)), pl.BlockSpec((t, KV_W), lambda b: (b, 0))],
        out_specs=pl.BlockSpec((t, 2 * KV_W), lambda b: (b, 0)),
        out_shape=jax.ShapeDtypeStruct((rows, 2 * KV_W), BF16),
        compiler_params=_params(1),
        name="fox_prep",
    )(lft, kb)


KEY_NEG_INF = INT_MIN + 0x7FFFFF


def _threshold(key):
    bits = key ^ ((key >> 31) & 0x7FFFFFFF)
    return jnp.where(key <= KEY_NEG_INF, -jnp.inf, pltpu.bitcast(bits, F32))


def _tree_reduce(parts, fn):
    while len(parts) > 1:
        parts = [fn(parts[j], parts[j + 1]) for j in range(0, len(parts) - 1, 2)] + (
            parts[-1:] if len(parts) % 2 else [])
    return parts[0]


def _kth_largest(count_ge, k, n_all, bits_per_step=1):
    kf = float(k)
    zero_key = jnp.zeros(n_all.shape, I32)
    c0 = count_ge(_threshold(zero_key))
    nonneg = c0 >= kf
    state = (jnp.where(nonneg, zero_key, INT_MIN), jnp.where(nonneg, c0, n_all))

    def step(state, shift, n_bits):
        prefix, n_ge = state
        new_prefix, new_ge = prefix, n_ge
        for j in range(1, 2 ** n_bits):
            cand = prefix + lax.shift_left(jnp.int32(j), shift)
            c = count_ge(_threshold(cand))
            keep = c >= kf
            new_prefix, new_ge = jnp.where(keep, cand, new_prefix), jnp.where(keep, c, new_ge)
        return new_prefix, new_ge

    low_bits = 31
    n_steps, rest = divmod(low_bits, bits_per_step)
    state = lax.fori_loop(
        0, n_steps, lambda it, st: step(st, low_bits - bits_per_step * (it + 1), bits_per_step), state)
    if rest:
        state = step(state, 0, rest)
    return _threshold(state[0]), state[1]


def _tie_cut(count_tie_below, need, n_bits):
    def body(it, j):
        cand = j + lax.shift_left(jnp.int32(1), n_bits - 1 - it)
        return jnp.where(count_tie_below(cand) < need, cand, j)

    return lax.fori_loop(0, n_bits, body, jnp.zeros(need.shape, I32))


def _stage_major(n_items, stages):
    for stage in stages:
        for item in range(n_items):
            stage(item)


def _attn_scratch(tq, q_width):
    cols = GROUP * tq
    return [pltpu.VMEM((N_KV_HEADS, cols, q_width), BF16), pltpu.VMEM((N_KV_HEADS, 1, cols), F32),
            pltpu.VMEM((N_KV_HEADS, 1, cols), F32), pltpu.VMEM((N_KV_HEADS, LANES, cols), F32)]


def _attn_init(m_scr, l_scr, acc_scr):
    m_scr[...] = jnp.full(m_scr.shape, NEG, F32)
    l_scr[...] = jnp.zeros(l_scr.shape, F32)
    acc_scr[...] = jnp.zeros(acc_scr.shape, F32)


def _attn_chunk(logits_fn, vt_fn, m_scr, l_scr, acc_scr):
    s_all, p_all, a_all = {}, {}, {}

    def logits(kv):
        s_all[kv] = logits_fn(kv)

    def softmax(kv):
        s = s_all.pop(kv)
        m_old = m_scr[kv]
        mn = jnp.maximum(m_old, jnp.max(s, axis=0, keepdims=True))
        a_all[kv] = jnp.exp2(m_old - mn)
        p = jnp.exp2(s - mn)
        l_scr[kv] = a_all[kv] * l_scr[kv] + jnp.sum(p, axis=0, keepdims=True)
        m_scr[kv] = mn
        p_all[kv] = p.astype(BF16)

    def weighted_values(kv):
        acc_scr[kv] = a_all.pop(kv) * acc_scr[kv] + _dot(vt_fn(kv), p_all.pop(kv))

    _stage_major(N_KV_HEADS, (logits, softmax, weighted_values))


def _attn_finish(o_ref, gz_ref, l_scr, acc_scr, tq):
    for kv in range(N_KV_HEADS):
        o = (acc_scr[kv] / l_scr[kv]).T
        for g in range(GROUP):
            c0 = (kv * GROUP + g) * LANES
            o_ref[:, c0:c0 + LANES] = (o[g * tq:(g + 1) * tq] * gz_ref[:, c0:c0 + LANES].astype(F32)).astype(BF16)


def _dsa_prompt_kernel(q_ref, qi_ref, wit_ref, gz_ref, k_ref, vt_ref, kia_ref, kib_ref, o_ref, s_scr, b_scr,
                       q2_scr, m_scr, l_scr, acc_scr, *, tq, ck, topk, n_bits):
    i = pl.program_id(1)
    n_ck = ((i + 1) * tq + ck - 1) // ck
    kpos = lax.broadcasted_iota(I32, (ck, tq), 0)
    qpos = i * tq + lax.broadcasted_iota(I32, (ck, tq), 1)
    wit = wit_ref[...]

    def score_chunk(c, carry):
        ka = kia_ref[pl.ds(c * ck, ck), :]
        kb = kib_ref[pl.ds(c * ck, ck), :]
        acc = None
        for p in range(IDX_HEADS // 2):
            slab = qi_ref[:, p * LANES:(p + 1) * LANES]
            for u, kk in enumerate((ka, kb)):
                hh = 2 * p + u
                term = jnp.maximum(_dot_nt(kk, slab), 0.0) * wit[hh:hh + 1, :]
                acc = term if acc is None else acc + term
        s_scr[c] = jnp.where(c * ck + kpos <= qpos, acc, -jnp.inf)
        return carry

    lax.fori_loop(0, n_ck, score_chunk, 0)

    def count(pred):
        def body(c, acc):
            m = jnp.where(pred(s_scr[c], c * ck + kpos), 1.0, 0.0)
            return acc + _tree_reduce([m[r:r + SUBLANES] for r in range(0, ck, SUBLANES)], jnp.add)

        acc = lax.fori_loop(0, n_ck, body, jnp.zeros((SUBLANES, tq), F32))
        return jnp.sum(acc, axis=0, keepdims=True)

    n_all = jnp.zeros((1, tq), F32) + (n_ck * ck).astype(F32)
    v, n_ge = _kth_largest(lambda cand: count(lambda s, pos: s >= cand), topk, n_all)
    has_tie = jnp.max(jnp.where((n_ge > float(topk)) & (v > -jnp.inf), 1.0, 0.0)) > 0.5

    def tie_cut():
        need = float(topk) - count(lambda s, pos: s > v)
        return _tie_cut(lambda cand: count(lambda s, pos: (s == v) & (pos < cand)), need, n_bits)

    jcut = lax.cond(has_tie, tie_cut, lambda: jnp.full((1, tq), 2 ** n_bits, I32))

    def bias_chunk(c, carry):
        s = s_scr[c]
        sel = ((s > v) | ((s == v) & (c * ck + kpos <= jcut))) & (s > -jnp.inf)
        b_scr[c] = jnp.where(sel, 0.0, NEG)
        return carry

    lax.fori_loop(0, n_ck, bias_chunk, 0)

    for kv in range(N_KV_HEADS):
        for g in range(GROUP):
            h = kv * GROUP + g
            q2_scr[kv, g * tq:(g + 1) * tq, :] = q_ref[:, h * LANES:(h + 1) * LANES]
    _attn_init(m_scr, l_scr, acc_scr)

    def att_chunk(c, carry):
        b = b_scr[c]
        b2 = jnp.concatenate([b] * GROUP, axis=1)

        def logits_fn(kv):
            return _dot_nt(k_ref[pl.ds(c * ck, ck), kv * LANES:(kv + 1) * LANES], q2_scr[kv]) + b2

        _attn_chunk(logits_fn, lambda kv: vt_ref[c, kv * LANES:(kv + 1) * LANES, :], m_scr, l_scr, acc_scr)
        return carry

    lax.fori_loop(0, n_ck, att_chunk, 0)
    _attn_finish(o_ref, gz_ref, l_scr, acc_scr, tq)


def _dsa_prompt_call(q, qi, wit, gz, kb, vt, kia, kib, *, n_batch):
    rows = q.shape[0]
    t = rows // n_batch
    tq = min(128, t)
    ck = vt.shape[2]
    n_c = t // ck
    n_q = t // tq
    topk = min(TOPK_MAX, t // 4)
    n_bits = max(1, int(math.ceil(math.log2(t))))
    qrow = lambda w: pl.BlockSpec((tq, w), lambda b, i: (b * n_q + i, 0))
    seq = lambda w: pl.BlockSpec((t, w), lambda b, i: (b, 0))
    return pl.pallas_call(
        functools.partial(_dsa_prompt_kernel, tq=tq, ck=ck, topk=topk, n_bits=n_bits),
        grid=(n_batch, n_q),
        in_specs=[qrow(ATT_W), qrow(IDX_HEADS * IDX_DIM), pl.BlockSpec((IDX_HEADS, tq), lambda b, i: (0, b * n_q + i)),
                  qrow(ATT_W), seq(KV_W), pl.BlockSpec((n_c, KV_W, ck), lambda b, i: (b, 0, 0)),
                  seq(LANES), seq(LANES)],
        out_specs=qrow(ATT_W),
        out_shape=jax.ShapeDtypeStruct((rows, ATT_W), BF16),
        scratch_shapes=[pltpu.VMEM((n_c, ck, tq), F32), pltpu.VMEM((n_c, ck, tq), F32)] + _attn_scratch(tq, LANES),
        compiler_params=_params(2),
        name="dsa_prompt_attn",
    )(q, qi, wit, gz, kb, vt, kia, kib)


def _fox_prompt_kernel(q_ref, gz_ref, kx_ref, vt_ref, o_ref, q2_scr, m_scr, l_scr, acc_scr, *, tq, ck):
    i = pl.program_id(1)
    n_ck = ((i + 1) * tq + ck - 1) // ck
    lane = lax.broadcasted_iota(I32, (tq, LANES), 1)
    for kv in range(N_KV_HEADS):
        for g in range(GROUP):
            h = kv * GROUP + g
            ones = jnp.where(((lane & (N_HEADS - 1)) == h) & (lane < BIAS_PARTS * N_HEADS), 1.0, 0.0)
            q2_scr[kv, g * tq:(g + 1) * tq, 0:LANES] = q_ref[:, h * LANES:(h + 1) * LANES]
            q2_scr[kv, g * tq:(g + 1) * tq, LANES:2 * LANES] = ones.astype(BF16)
    _attn_init(m_scr, l_scr, acc_scr)

    def chunk(c, last):
        if last:
            key = c * ck + lax.broadcasted_iota(I32, (ck, GROUP * tq), 0)
            qpos = i * tq + (lax.broadcasted_iota(I32, (ck, GROUP * tq), 1) & (tq - 1))
            causal = key <= qpos

        def logits_fn(kv):
            kc = kx_ref[pl.ds(c * ck, ck), 2 * kv * LANES:(2 * kv + 2) * LANES]
            s = _dot_nt(kc, q2_scr[kv])
            return jnp.where(causal, s, NEG) if last else s

        _attn_chunk(logits_fn, lambda kv: vt_ref[c, kv * LANES:(kv + 1) * LANES, :], m_scr, l_scr, acc_scr)

    def body(c, carry):
        chunk(c, False)
        return carry

    lax.fori_loop(0, n_ck - 1, body, 0)
    chunk(n_ck - 1, True)
    _attn_finish(o_ref, gz_ref, l_scr, acc_scr, tq)


def _fox_prompt_call(q, gz, kx, vt, *, n_batch, tq):
    rows = q.shape[0]
    t = rows // n_batch
    ck = vt.shape[2]
    n_c = t // ck
    n_q = t // tq
    assert tq & (tq - 1) == 0 and BIAS_PARTS * N_HEADS <= LANES
    qrow = lambda w: pl.BlockSpec((tq, w), lambda b, i: (b * n_q + i, 0))
    return pl.pallas_call(
        functools.partial(_fox_prompt_kernel, tq=tq, ck=ck),
        grid=(n_batch, n_q),
        in_specs=[qrow(ATT_W), qrow(ATT_W), pl.BlockSpec((t, 2 * KV_W), lambda b, i: (b, 0)),
                  pl.BlockSpec((n_c, KV_W, ck), lambda b, i: (b, 0, 0))],
        scratch_shapes=_attn_scratch(tq, 2 * LANES),
        out_specs=qrow(ATT_W),
        out_shape=jax.ShapeDtypeStruct((rows, ATT_W), BF16),
        compiler_params=_params(2),
        name="fox_prompt_attn",
    )(q, gz, kx, vt)


def _group_rows(x0, x1, row):
    return jnp.where(row < SAMPLE_ROWS // 2, x0, pltpu.roll(x1, SAMPLE_ROWS // 2, 0))


def _page_block(new_ref):
    new = new_ref[0]
    return jnp.concatenate([new, jnp.zeros((PAGE_SIZE - SAMPLE_ROWS, new.shape[1]), new.dtype)], axis=0)


def _kv_pages(page_refs, new_ref, kv):
    pages = [r[0, 0, pl.ds(kv, PAGE_SIZE, stride=N_KV_HEADS), :].astype(BF16) for r in page_refs]
    new = new_ref[0, pl.ds(kv, SAMPLE_ROWS, stride=N_KV_HEADS), :]
    new = jnp.concatenate([new, jnp.zeros((PAGE_SIZE - SAMPLE_ROWS, LANES), new.dtype)], axis=0)
    return pages + [new.astype(BF16)]


def _sample_attend(q_ref, gz_ref, k_refs, kn_ref, v_refs, vn_ref, bias_fn, o_ref):
    row = lax.broadcasted_iota(I32, (SAMPLE_ROWS, LANES), 0)
    real = row < SAMPLE_ROWS // 2
    qf = q_ref[0].astype(F32)
    gz = gz_ref[0].astype(F32)
    for kv in range(N_KV_HEADS):
        h0 = kv * GROUP
        sl = lambda a, h: a[:, h * LANES:(h + 1) * LANES]
        q2 = _group_rows(sl(qf, h0), sl(qf, h0 + 1), row).astype(BF16)
        s = jnp.concatenate([_dot_nt(q2, kp) for kp in _kv_pages(k_refs, kn_ref, kv)], axis=1)
        s = s + bias_fn(kv)
        m = jnp.max(s, axis=1, keepdims=True)
        p = jnp.exp2(s - m)
        l = jnp.sum(p, axis=1, keepdims=True)
        pb = p.astype(BF16)
        o = None
        for n, vp in enumerate(_kv_pages(v_refs, vn_ref, kv)):
            t = _dot(pb[:, n * LANES:(n + 1) * LANES], vp)
            o = t if o is None else o + t
        o = o / l
        o_ref[0, :, h0 * LANES:(h0 + 1) * LANES] = jnp.where(real, o * sl(gz, h0), 0.0).astype(BF16)
        o_ref[0, :, (h0 + 1) * LANES:(h0 + 2) * LANES] = jnp.where(
            real, pltpu.roll(o, SAMPLE_ROWS // 2, 0) * sl(gz, h0 + 1), 0.0).astype(BF16)


def _dsa_sample_kernel(pt_ref, q_ref, qi_ref, wi_ref, gz_ref, kn_ref, vn_ref, kin_ref, *rest,
                       n_pages, topk, n_bits):
    k_refs, v_refs, ki_refs = rest[:n_pages], rest[n_pages:2 * n_pages], rest[2 * n_pages:3 * n_pages]
    o_ref = rest[3 * n_pages]
    n_chunks = n_pages + 1
    n_keys = n_chunks * PAGE_SIZE
    past = n_pages * PAGE_SIZE
    qi = qi_ref[0]
    wi = wi_ref[0]

    def score(dots):
        t = jnp.maximum(dots, 0.0) * wi
        return jnp.sum(t.reshape(SAMPLE_ROWS, IDX_HEADS, PAGE_SIZE), axis=1)

    chunks = [score(_dot(qi, r[0, 0].astype(BF16))) for r in ki_refs]
    chunks.append(score(_dot_nt(qi, _page_block(kin_ref).astype(BF16))))
    score_all = jnp.concatenate(chunks, axis=1)
    row = lax.broadcasted_iota(I32, (SAMPLE_ROWS, n_keys), 0)
    col = lax.broadcasted_iota(I32, (SAMPLE_ROWS, n_keys), 1)
    causal = col <= past + row
    s = jnp.where(causal, score_all, -jnp.inf)

    count = lambda m: jnp.sum(jnp.where(m, 1.0, 0.0), axis=1, keepdims=True)
    v, n_ge = _kth_largest(lambda cand: count(s >= cand), topk, jnp.full((SAMPLE_ROWS, 1), float(n_keys), F32),
                           bits_per_step=2)
    has_tie = jnp.max(jnp.where((n_ge > float(topk)) & (v > -jnp.inf), 1.0, 0.0)) > 0.5

    def tie_cut():
        need = float(topk) - count(s > v)
        return _tie_cut(lambda cand: count((s == v) & (col < cand)), need, n_bits)

    jcut = lax.cond(has_tie, tie_cut, lambda: jnp.full((SAMPLE_ROWS, 1), 2 ** n_bits, I32))
    sel = ((s > v) | ((s == v) & (col <= jcut))) & causal
    bias = jnp.where(sel, 0.0, NEG)
    bias2 = jnp.where(row < SAMPLE_ROWS // 2, bias, pltpu.roll(bias, SAMPLE_ROWS // 2, 0))

    _sample_attend(q_ref, gz_ref, k_refs, kn_ref, v_refs, vn_ref, lambda kv: bias2, o_ref)


def _fox_sample_kernel(pt_ref, q_ref, gz_ref, kn_ref, vn_ref, lfn_ref, *rest, n_pages):
    k_refs, v_refs, lf_refs = rest[:n_pages], rest[n_pages:2 * n_pages], rest[2 * n_pages:3 * n_pages]
    o_ref = rest[3 * n_pages]
    n_keys = (n_pages + 1) * PAGE_SIZE
    past = n_pages * PAGE_SIZE
    lf_t = [r[0, 0] for r in lf_refs]
    c_all = _lane_cumsum(jnp.concatenate(lf_t + [lfn_ref[0]], axis=1)) * LOG2E
    row = lax.broadcasted_iota(I32, (SAMPLE_ROWS, n_keys), 0)
    col = lax.broadcasted_iota(I32, (SAMPLE_ROWS, n_keys), 1)
    tok = row & (SAMPLE_ROWS // 2 - 1)
    mask = jnp.where(col <= past + tok, 0.0, NEG)

    def bias_fn(kv):
        h0 = kv * GROUP
        c2 = jnp.where(row < SAMPLE_ROWS // 2, c_all[h0:h0 + 1, :], c_all[h0 + 1:h0 + 2, :])
        return mask - c2

    _sample_attend(q_ref, gz_ref, k_refs, kn_ref, v_refs, vn_ref, bias_fn, o_ref)


def _page_specs(layer, n_pages, rows, width):
    def spec(p):
        return pl.BlockSpec((1, 1, rows, width), lambda b, pt: (layer, pt[b * n_pages + p], 0, 0))
    return [spec(p) for p in range(n_pages)]


def _kv_page_specs(layer, n_pages):
    return _page_specs(layer, n_pages, PAGE_SIZE * N_KV_HEADS, HEAD_DIM)


def _seq_spec(a):
    return pl.BlockSpec((1,) + a.shape[1:], lambda b, pt: (b,) + (0,) * (a.ndim - 1))


def _dsa_sample_call(pt_flat, q, qi, wi, gz, kn, vn, kin, cache_k, cache_v, cache_ki, *, layer, n_pages, dec_seq):
    n_seq = q.shape[0]
    n_keys = (n_pages + 1) * PAGE_SIZE
    topk = min(TOPK_MAX, (n_pages * PAGE_SIZE + dec_seq) // 4)
    n_bits = int(math.ceil(math.log2(n_keys)))
    seq_in = [q, qi, wi, gz, kn, vn, kin]
    grid_spec = pltpu.PrefetchScalarGridSpec(
        num_scalar_prefetch=1,
        grid=(n_seq,),
        in_specs=[_seq_spec(a) for a in seq_in]
        + _kv_page_specs(layer, n_pages) + _kv_page_specs(layer, n_pages)
        + _page_specs(layer, n_pages, IDX_DIM, PAGE_SIZE),
        out_specs=pl.BlockSpec((1, SAMPLE_ROWS, ATT_W), lambda b, pt: (b, 0, 0)),
    )
    return pl.pallas_call(
        functools.partial(_dsa_sample_kernel, n_pages=n_pages, topk=topk, n_bits=n_bits),
        grid_spec=grid_spec,
        out_shape=jax.ShapeDtypeStruct((n_seq, SAMPLE_ROWS, ATT_W), BF16),
        compiler_params=_params(1),
        name="dsa_sample_attn",
    )(pt_flat, *seq_in, *([cache_k] * n_pages), *([cache_v] * n_pages), *([cache_ki] * n_pages))


def _fox_sample_call(pt_flat, q, gz, kn, vn, lfn, cache_k, cache_v, cache_lf, *, layer, n_pages):
    n_seq = q.shape[0]
    seq_in = [q, gz, kn, vn, lfn]
    grid_spec = pltpu.PrefetchScalarGridSpec(
        num_scalar_prefetch=1,
        grid=(n_seq,),
        in_specs=[_seq_spec(a) for a in seq_in]
        + _kv_page_specs(layer, n_pages) + _kv_page_specs(layer, n_pages)
        + _page_specs(layer, n_pages, N_HEADS, PAGE_SIZE),
        out_specs=pl.BlockSpec((1, SAMPLE_ROWS, ATT_W), lambda b, pt: (b, 0, 0)),
    )
    return pl.pallas_call(
        functools.partial(_fox_sample_kernel, n_pages=n_pages),
        grid_spec=grid_spec,
        out_shape=jax.ShapeDtypeStruct((n_seq, SAMPLE_ROWS, ATT_W), BF16),
        compiler_params=_params(1),
        name="fox_sample_attn",
    )(pt_flat, *seq_in, *([cache_k] * n_pages), *([cache_v] * n_pages), *([cache_lf] * n_pages))


def _rope_tables(pos, dim, rot_dim):
    half = rot_dim // 2
    inv = ROPE_THETA ** (-jnp.arange(half, dtype=F32) / half)
    ang = pos.astype(F32)[:, None] * inv[None, :]
    cos, sin = jnp.cos(ang), jnp.sin(ang)
    n = pos.shape[0]
    cos_t = jnp.concatenate([cos, cos, jnp.ones((n, dim - rot_dim), F32)], axis=1)
    sin_t = jnp.concatenate([-sin, sin, jnp.zeros((n, dim - rot_dim), F32)], axis=1)
    rep = LANES // dim
    return jnp.tile(cos_t, (1, rep)), jnp.tile(sin_t, (1, rep))


def kernel(x_prompt, x_sample, cache_dsa_k, cache_dsa_v, cache_dsa_kidx, cache_fox_k, cache_fox_v, cache_fox_logf,
           page_table, c_prompt, c_sample, norm_g, w_ada, b_ada, w_in_dsa, w_out_dsa, w_in_fox, b_forget,
           w_out_fox, final_g):
    n_b, t_p, d = x_prompt.shape
    n_s, t_s, _ = x_sample.shape
    depth = norm_g.shape[0]
    n_pages = page_table.shape[1]
    past = n_pages * PAGE_SIZE
    n_pool = cache_dsa_k.shape[1]
    assert t_s <= SAMPLE_ROWS // 2 and d == ATT_W

    n_c = n_b + n_s
    n_cp = -(-n_c // SUBLANES) * SUBLANES
    c_all = jnp.concatenate([c_prompt, c_sample, jnp.zeros((n_cp - n_c, d), F32)], axis=0)
    mod = _ada_call(c_all, w_ada, b_ada)

    pos_p = jnp.arange(t_p)
    pos_s = jnp.tile(past + jnp.arange(SAMPLE_ROWS), n_s)
    tabs_p = _rope_tables(pos_p, HEAD_DIM, ROT_DIM) + _rope_tables(pos_p, IDX_DIM, IDX_ROT_DIM)
    tabs_s = _rope_tables(pos_s, HEAD_DIM, ROT_DIM) + _rope_tables(pos_s, IDX_DIM, IDX_ROT_DIM)

    rows_p = n_b * t_p
    rows_s = n_s * SAMPLE_ROWS
    tm_p = _row_tile(t_p)
    tpb = t_p // tm_p
    tm_s = _row_tile(rows_s)
    xp = x_prompt.reshape(rows_p, d)
    xs = jnp.pad(x_sample, ((0, 0), (0, SAMPLE_ROWS - t_s), (0, 0))).reshape(rows_s, d)
    pt_flat = page_table.reshape(-1).astype(I32)

    page_view = lambda c: c.reshape(c.shape[0], n_pool, PAGE_SIZE * N_KV_HEADS, HEAD_DIM)
    ck_dsa, cv_dsa, ck_fox, cv_fox = (page_view(c) for c in (cache_dsa_k, cache_dsa_v, cache_fox_k, cache_fox_v))
    cki_dsa = jnp.swapaxes(cache_dsa_kidx, 2, 3)
    clf_fox = jnp.swapaxes(cache_fox_logf, 2, 3)

    fg = final_g.reshape(1, d)
    tq_fox = min(128, t_p)
    outs = {k: [] for k in ("dk_p", "dv_p", "dki_p", "fk_p", "fv_p", "fl_p", "dk_s", "dv_s", "dki_s", "fk_s", "fv_s", "fl_s")}

    h3 = lambda a: a.reshape(n_s, SAMPLE_ROWS * N_KV_HEADS, HEAD_DIM)
    h4 = lambda a: a.reshape(n_s, SAMPLE_ROWS, N_KV_HEADS, HEAD_DIM)

    def per_row(a):
        return jnp.repeat(a, SAMPLE_ROWS, axis=0).reshape(rows_s // tm_s, tm_s, d)

    for i in range(depth):
        j = i // N_MIXERS
        g = norm_g[i].reshape(1, d)
        shift_p, scale_p, gate_p = (mod[i, :n_b, k * d:(k + 1) * d].reshape(n_b, 1, d) for k in range(3))
        shift_s, scale_s, gate_s = (per_row(mod[i, n_b:n_c, k * d:(k + 1) * d]) for k in range(3))
        final = i == depth - 1
        if i % N_MIXERS == 0:
            w = w_in_dsa[j]
            n_main = 2 * ATT_W + 2 * KV_W + IDX_HEADS * IDX_DIM
            w_main = w[:, :n_main].astype(BF16)
            w_tail = jnp.pad(w[:, n_main:], ((0, 0), (0, LANES - (w.shape[1] - n_main)))).astype(BF16)
            w_vt = w[:, ATT_W + KV_W:ATT_W + 2 * KV_W].T.astype(BF16)
            w_wit = jnp.pad(w[:, n_main + IDX_DIM:].T, ((0, 2 * SUBLANES - IDX_HEADS), (0, 0))).astype(BF16)
            w_out = w_out_dsa[j].astype(BF16)
            q, kf, kb, vf, vt, gz, qi, kif, kia, kib, wi, wit = _inproj_dsa_call(
                xp, scale_p, shift_p, g, tabs_p, w_main, w_tail, w_vt, w_wit, tiles_per_batch=tpb, per_row=False)
            og = _dsa_prompt_call(q, qi, wit, gz, kb, vt, kia, kib, n_batch=n_b)
            xp = _outproj_call(og, xp, gate_p, w_out, fg, tiles_per_batch=tpb, per_row=False, final=final)
            outs["dk_p"].append(kf.reshape(n_b, t_p, N_KV_HEADS, HEAD_DIM))
            outs["dv_p"].append(vf.reshape(n_b, t_p, N_KV_HEADS, HEAD_DIM))
            outs["dki_p"].append(kif.reshape(n_b, t_p, IDX_DIM))
            q, kf, kb, vf, vt, gz, qi, kif, kia, kib, wi, wit = _inproj_dsa_call(
                xs, scale_s, shift_s, g, tabs_s, w_main, w_tail, w_vt, w_wit, tiles_per_batch=1, per_row=True)
            r3 = lambda a: a.reshape(n_s, SAMPLE_ROWS, a.shape[-1])
            og = _dsa_sample_call(
                pt_flat, r3(q), qi.reshape(n_s, SAMPLE_ROWS * IDX_HEADS, IDX_DIM),
                wi.reshape(n_s, SAMPLE_ROWS * IDX_HEADS, 1), r3(gz), h3(kf), h3(vf), r3(kif),
                ck_dsa, cv_dsa, cki_dsa, layer=j, n_pages=n_pages, dec_seq=t_s)
            xs = _outproj_call(og.reshape(rows_s, ATT_W), xs, gate_s, w_out, fg, tiles_per_batch=1, per_row=True,
                               final=final)
            outs["dk_s"].append(h4(kf)[:, :t_s])
            outs["dv_s"].append(h4(vf)[:, :t_s])
            outs["dki_s"].append(r3(kif)[:, :t_s])
        else:
            w = w_in_fox[j]
            n_main = 2 * ATT_W + 2 * KV_W
            w_main = w[:, :n_main].astype(BF16)
            w_vt = w[:, ATT_W + KV_W:ATT_W + 2 * KV_W].T.astype(BF16)
            w_flt = jnp.pad(w[:, n_main:].T, ((0, 2 * SUBLANES - N_HEADS), (0, 0))).astype(BF16)
            b_f = b_forget[j].reshape(N_HEADS, 1)
            w_out = w_out_fox[j].astype(BF16)
            q, kf, kb, vf, vt, gz, lft = _inproj_fox_call(
                xp, scale_p, shift_p, g, w_main, w_vt, w_flt, b_f, tiles_per_batch=tpb, per_row=False)
            kx = _fox_prep_call(lft, kb, n_b)
            og = _fox_prompt_call(q, gz, kx, vt, n_batch=n_b, tq=tq_fox)
            xp = _outproj_call(og, xp, gate_p, w_out, fg, tiles_per_batch=tpb, per_row=False, final=final)
            outs["fk_p"].append(kf.reshape(n_b, t_p, N_KV_HEADS, HEAD_DIM))
            outs["fv_p"].append(vf.reshape(n_b, t_p, N_KV_HEADS, HEAD_DIM))
            outs["fl_p"].append(lft.T.reshape(n_b, t_p, N_HEADS))
            q, kf, kb, vf, vt, gz, lft = _inproj_fox_call(
                xs, scale_s, shift_s, g, w_main, w_vt, w_flt, b_f, tiles_per_batch=1, per_row=True)
            r3 = lambda a: a.reshape(n_s, SAMPLE_ROWS, a.shape[-1])
            lf3 = lft.reshape(N_HEADS, n_s, SAMPLE_ROWS).transpose(1, 0, 2)
            lfn = jnp.pad(lf3, ((0, 0), (0, 0), (0, PAGE_SIZE - SAMPLE_ROWS)))
            og = _fox_sample_call(pt_flat, r3(q), r3(gz), h3(kf), h3(vf), lfn, ck_fox, cv_fox, clf_fox,
                                  layer=j, n_pages=n_pages)
            xs = _outproj_call(og.reshape(rows_s, ATT_W), xs, gate_s, w_out, fg, tiles_per_batch=1, per_row=True,
                               final=final)
            outs["fk_s"].append(h4(kf)[:, :t_s])
            outs["fv_s"].append(h4(vf)[:, :t_s])
            outs["fl_s"].append(lf3.transpose(0, 2, 1)[:, :t_s])

    y_prompt = xp.reshape(n_b, t_p, d)
    y_sample = xs.reshape(n_s, SAMPLE_ROWS, d)[:, :t_s]
    st = lambda k: jnp.stack(outs[k])
    return (y_prompt, y_sample, st("dk_p"), st("dv_p"), st("dki_p"), st("fk_p"), st("fv_p"), st("fl_p"),
            st("dk_s"), st("dv_s"), st("dki_s"), st("fk_s"), st("fv_s"), st("fl_s"))
```

```python
import functools
import math

import jax
import jax.numpy as jnp
import numpy as np
from jax import lax
from jax.experimental import pallas as pl
from jax.experimental.pallas import tpu as pltpu

N_HEADS = 8
HEAD_DIM = 128
N_KV_HEADS = 4
GROUP = N_HEADS // N_KV_HEADS
ROT_DIM = HEAD_DIM // 4
ROPE_THETA = 500000.0
IDX_HEADS = 8
IDX_DIM = 64
IDX_ROT_DIM = IDX_DIM // 4
TOPK_MAX = 256
PAGE_SIZE = 128
EPS = 1e-6
ATT_W = N_HEADS * HEAD_DIM
KV_W = N_KV_HEADS * HEAD_DIM
N_MIXERS = 2

LANES = 128
SUBLANES = 8
VMEM_LIMIT = 56 * 1024 * 1024

SAMPLE_ROWS = SUBLANES
NEG = -1e30
LOG2E = math.log2(math.e)
INT_MIN = -(2 ** 31)

F32 = jnp.float32
BF16 = jnp.bfloat16
I32 = jnp.int32


def _params(n_axes):
    return pltpu.CompilerParams(dimension_semantics=("arbitrary",) * n_axes, vmem_limit_bytes=VMEM_LIMIT)


def _dot(a, b):
    return jnp.dot(a, b, preferred_element_type=F32)


def _dot_nt(a, b):
    return lax.dot_general(a, b, (((1,), (1,)), ((), ())), preferred_element_type=F32)


def _sigmoid(x):
    return 1.0 / (1.0 + jnp.exp(-x))


def _ada_kernel(c_ref, w_ref, b_ref, o_ref):
    c = c_ref[...].astype(BF16)
    w = w_ref[0].astype(BF16)
    o_ref[0] = _dot(c, w) + b_ref[0]


def _ada_call(c_all, w_ada, b_ada):
    depth, d, n3 = w_ada.shape
    m = c_all.shape[0]
    tn = d
    return pl.pallas_call(
        _ada_kernel,
        grid=(depth, n3 // tn),
        in_specs=[
            pl.BlockSpec((m, d), lambda l, j: (0, 0)),
            pl.BlockSpec((1, d, tn), lambda l, j: (l, 0, j)),
            pl.BlockSpec((1, 1, tn), lambda l, j: (l, 0, j)),
        ],
        out_specs=pl.BlockSpec((1, m, tn), lambda l, j: (l, 0, j)),
        out_shape=jax.ShapeDtypeStruct((depth, m, n3), F32),
        compiler_params=_params(2),
        name="ada_mod",
    )(c_all, w_ada, b_ada.reshape(depth, 1, n3))


def _modulated_input(x_ref, sc_ref, sh_ref, g_ref):
    x = x_ref[...]
    ms = jnp.mean(x * x, axis=-1, keepdims=True)
    y = x * lax.rsqrt(ms + EPS) * g_ref[...]
    return (y * (1.0 + sc_ref[0]) + sh_ref[0]).astype(BF16)


def _store_head(ref, head, value):
    ref[pl.ds(head, value.shape[0], stride=N_KV_HEADS), :] = value


def _head_rows_spec(tm):
    return pl.BlockSpec((tm * N_KV_HEADS, HEAD_DIM), lambda i: (i, 0))


def _rope(v, lane, cos_t, sin_t, period, half):
    vr = jnp.where((lane & (period - 1)) < half, pltpu.roll(v, LANES - half, 1), pltpu.roll(v, half, 1))
    return v * cos_t + vr * sin_t


def _inproj_dsa_kernel(x_ref, sc_ref, sh_ref, g_ref, c128_ref, s128_ref, c64_ref, s64_ref, w_ref, wt_ref,
                       wvt_ref, wwt_ref,
                       q_ref, kf_ref, kb_ref, vf_ref, vt_ref, gz_ref, qi_ref, qis_ref, kif_ref, kia_ref, wi_ref,
                       wit_ref, *, q_scale, wi_scale):
    h = _modulated_input(x_ref, sc_ref, sh_ref, g_ref)
    tm = h.shape[0]
    lane = lax.broadcasted_iota(I32, (tm, LANES), 1)
    c128, s128, c64, s64 = c128_ref[...], s128_ref[...], c64_ref[...], s64_ref[...]
    rope_h = lambda v: _rope(v, lane, c128, s128, HEAD_DIM, ROT_DIM // 2)
    rope_i = lambda v: _rope(v, lane, c64, s64, IDX_DIM, IDX_ROT_DIM // 2)
    seg = 4 * LANES
    off = 0
    for s in range(ATT_W // seg):
        r = _dot(h, w_ref[:, off:off + seg])
        for j in range(seg // LANES):
            c0 = s * seg + j * LANES
            q_ref[:, c0:c0 + LANES] = (rope_h(r[:, j * LANES:(j + 1) * LANES]) * q_scale).astype(BF16)
        off += seg
    r = _dot(h, w_ref[:, off:off + KV_W])
    for j in range(N_KV_HEADS):
        kr = rope_h(r[:, j * LANES:(j + 1) * LANES])
        _store_head(kf_ref, j, kr)
        kb_ref[:, j * LANES:(j + 1) * LANES] = kr.astype(BF16)
    off += KV_W
    r = _dot(h, w_ref[:, off:off + KV_W])
    for j in range(N_KV_HEADS):
        _store_head(vf_ref, j, r[:, j * LANES:(j + 1) * LANES])
    vt_ref[0] = _dot_nt(wvt_ref[...], h).astype(BF16)
    off += KV_W
    for s in range(ATT_W // seg):
        r = _dot(h, w_ref[:, off:off + seg])
        gz_ref[:, s * seg:(s + 1) * seg] = (r * _sigmoid(r)).astype(BF16)
        off += seg
    r = _dot(h, w_ref[:, off:off + IDX_HEADS * IDX_DIM])
    for j in range(IDX_HEADS * IDX_DIM // LANES):
        pair = rope_i(r[:, j * LANES:(j + 1) * LANES])
        qi_ref[:, j * LANES:(j + 1) * LANES] = pair.astype(BF16)
        qis_ref[:, j * LANES:(j + 1) * LANES] = pltpu.roll(pair, IDX_DIM, 1).astype(BF16)
    t = _dot(h, wt_ref[...])
    tr = rope_i(t)
    kif_ref[...] = tr[:, 0:IDX_DIM]
    kia_ref[...] = jnp.where(lane < IDX_DIM, tr, 0.0).astype(BF16)
    wi_ref[...] = t[:, IDX_DIM:IDX_DIM + IDX_HEADS] * wi_scale
    wit_ref[...] = _dot_nt(wwt_ref[...], h)[0:IDX_HEADS, :] * wi_scale


def _inproj_fox_kernel(x_ref, sc_ref, sh_ref, g_ref, w_ref, wvt_ref, wfl_ref, bf_ref,
                       q_ref, kf_ref, kb_ref, vf_ref, vt_ref, gz_ref, lft_ref, *, q_scale):
    h = _modulated_input(x_ref, sc_ref, sh_ref, g_ref)
    seg = 4 * LANES
    off = 0
    for s in range(ATT_W // seg):
        r = _dot(h, w_ref[:, off:off + seg])
        q_ref[:, s * seg:(s + 1) * seg] = (r * q_scale).astype(BF16)
        off += seg
    r = _dot(h, w_ref[:, off:off + KV_W])
    for j in range(N_KV_HEADS):
        _store_head(kf_ref, j, r[:, j * LANES:(j + 1) * LANES])
    kb_ref[...] = r.astype(BF16)
    off += KV_W
    r = _dot(h, w_ref[:, off:off + KV_W])
    for j in range(N_KV_HEADS):
        _store_head(vf_ref, j, r[:, j * LANES:(j + 1) * LANES])
    vt_ref[0] = _dot_nt(wvt_ref[...], h).astype(BF16)
    off += KV_W
    for s in range(ATT_W // seg):
        r = _dot(h, w_ref[:, off:off + seg])
        gz_ref[:, s * seg:(s + 1) * seg] = (r * _sigmoid(r)).astype(BF16)
        off += seg
    z = _dot_nt(wfl_ref[...], h)[0:N_HEADS, :] + bf_ref[...]
    lft_ref[...] = jnp.minimum(z, 0.0) - jnp.log1p(jnp.exp(-jnp.abs(z)))


def _row_tile(rows):
    return 512 if rows % 512 == 0 else rows


def _mod_specs(tm, d, tiles_per_batch, per_row):
    if per_row:
        return pl.BlockSpec((1, tm, d), lambda i: (i, 0, 0))
    return pl.BlockSpec((1, 1, d), lambda i: (i // tiles_per_batch, 0, 0))


def _inproj_dsa_call(x, scale, shift, g, tabs, w_main, w_tail, w_vt, w_wit, *, tiles_per_batch, per_row):
    rows, d = x.shape
    tm = _row_tile(rows) if per_row else rows // (scale.shape[0] * tiles_per_batch)
    n_t = tabs[0].shape[0] // tm
    row = lambda w: pl.BlockSpec((tm, w), lambda i: (i, 0))
    tab = pl.BlockSpec((tm, LANES), lambda i: (i % n_t, 0))
    mod = _mod_specs(tm, d, tiles_per_batch, per_row)
    full = lambda a: pl.BlockSpec(a.shape, lambda i: (0,) * a.ndim)
    sd = jax.ShapeDtypeStruct
    heads_f32 = (_head_rows_spec(tm), sd((rows * N_KV_HEADS, HEAD_DIM), F32))
    outs = [(row(ATT_W), sd((rows, ATT_W), BF16)), heads_f32, (row(KV_W), sd((rows, KV_W), BF16)), heads_f32,
            (pl.BlockSpec((1, KV_W, tm), lambda i: (i, 0, 0)), sd((rows // tm, KV_W, tm), BF16)),
            (row(ATT_W), sd((rows, ATT_W), BF16)),
            (row(IDX_HEADS * IDX_DIM), sd((rows, IDX_HEADS * IDX_DIM), BF16)),
            (row(IDX_HEADS * IDX_DIM), sd((rows, IDX_HEADS * IDX_DIM), BF16)), (row(IDX_DIM), sd((rows, IDX_DIM), F32)),
            (row(LANES), sd((rows, LANES), BF16)),
            (row(IDX_HEADS), sd((rows, IDX_HEADS), F32)),
            (pl.BlockSpec((IDX_HEADS, tm), lambda i: (0, i)), sd((IDX_HEADS, rows), F32))]
    return pl.pallas_call(
        functools.partial(_inproj_dsa_kernel, q_scale=HEAD_DIM ** -0.5 * LOG2E,
                          wi_scale=IDX_HEADS ** -0.5 * IDX_DIM ** -0.5),
        grid=(rows // tm,),
        in_specs=[row(d), mod, mod, full(g), tab, tab, tab, tab, full(w_main), full(w_tail), full(w_vt), full(w_wit)],
        out_specs=[s for s, _ in outs],
        out_shape=[t for _, t in outs],
        compiler_params=_params(1),
        name="inproj_dsa",
    )(x, scale, shift, g, *tabs, w_main, w_tail, w_vt, w_wit)


def _inproj_fox_call(x, scale, shift, g, w_main, w_vt, w_flt, b_f, *, tiles_per_batch, per_row):
    rows, d = x.shape
    tm = _row_tile(rows) if per_row else rows // (scale.shape[0] * tiles_per_batch)
    row = lambda w: pl.BlockSpec((tm, w), lambda i: (i, 0))
    mod = _mod_specs(tm, d, tiles_per_batch, per_row)
    full = lambda a: pl.BlockSpec(a.shape, lambda i: (0,) * a.ndim)
    sd = jax.ShapeDtypeStruct
    return pl.pallas_call(
        functools.partial(_inproj_fox_kernel, q_scale=HEAD_DIM ** -0.5 * LOG2E),
        grid=(rows // tm,),
        in_specs=[row(d), mod, mod, full(g), full(w_main), full(w_vt), full(w_flt), full(b_f)],
        out_specs=[row(ATT_W), _head_rows_spec(tm), row(KV_W), _head_rows_spec(tm),
                   pl.BlockSpec((1, KV_W, tm), lambda i: (i, 0, 0)), row(ATT_W),
                   pl.BlockSpec((N_HEADS, tm), lambda i: (0, i))],
        out_shape=[sd((rows, ATT_W), BF16), sd((rows * N_KV_HEADS, HEAD_DIM), F32), sd((rows, KV_W), BF16),
                   sd((rows * N_KV_HEADS, HEAD_DIM), F32),
                   sd((rows // tm, KV_W, tm), BF16), sd((rows, ATT_W), BF16), sd((N_HEADS, rows), F32)],
        compiler_params=_params(1),
        name="inproj_fox",
    )(x, scale, shift, g, w_main, w_vt, w_flt, b_f)


def _outproj_kernel(og_ref, x_ref, gate_ref, w_ref, fg_ref, o_ref, *, final):
    xn = x_ref[...] + gate_ref[0] * _dot(og_ref[...], w_ref[...])
    if final:
        ms = jnp.mean(xn * xn, axis=-1, keepdims=True)
        xn = xn * lax.rsqrt(ms + EPS) * fg_ref[...]
    o_ref[...] = xn


def _outproj_call(og, x, gate, w_out, final_g, *, tiles_per_batch, per_row, final):
    rows, d = x.shape
    tm = _row_tile(rows) if per_row else rows // (gate.shape[0] * tiles_per_batch)
    row = lambda w: pl.BlockSpec((tm, w), lambda i: (i, 0))
    full = lambda a: pl.BlockSpec(a.shape, lambda i: (0,) * a.ndim)
    return pl.pallas_call(
        functools.partial(_outproj_kernel, final=final),
        grid=(rows // tm,),
        in_specs=[row(ATT_W), row(d), _mod_specs(tm, d, tiles_per_batch, per_row), full(w_out), full(final_g)],
        out_specs=row(d),
        out_shape=jax.ShapeDtypeStruct((rows, d), F32),
        compiler_params=_params(1),
        name="outproj",
    )(og, x, gate, w_out, final_g)


def _lane_cumsum(x):
    n = x.shape[-1]
    lane = lax.broadcasted_iota(I32, x.shape, x.ndim - 1)
    s = 1
    while s < n:
        x = x + jnp.where(lane >= s, pltpu.roll(x, s, x.ndim - 1), 0.0)
        s *= 2
    return x


BIAS_PARTS = 3


def _fox_prep_kernel(lft_ref, kb_ref, kx_ref):
    t = lft_ref.shape[1]
    c = _lane_cumsum(lft_ref[...]) * (-LOG2E)
    ct = jnp.concatenate([c, jnp.zeros((LANES - N_HEADS, t), F32)], axis=0).T
    ext = jnp.zeros((t, LANES), F32)
    rest = ct
    for part in range(BIAS_PARTS):
        piece = rest.astype(BF16).astype(F32)
        rest = rest - piece
        ext = ext + (piece if part == 0 else pltpu.roll(piece, part * N_HEADS, 1))
    ext = ext.astype(BF16)
    for kv in range(N_KV_HEADS):
        kx_ref[:, 2 * kv * LANES:(2 * kv + 1) * LANES] = kb_ref[:, kv * LANES:(kv + 1) * LANES]
        kx_ref[:, (2 * kv + 1) * LANES:(2 * kv + 2) * LANES] = ext


def _fox_prep_call(lft, kb, n_batch):
    h, rows = lft.shape
    t = rows // n_batch
    return pl.pallas_call(
        _fox_prep_kernel,
        grid=(n_batch,),
        in_specs=[pl.BlockSpec((h, t), lambda b: (0, b)), pl.BlockSpec((t, KV_W), lambda b: (b, 0))],
        out_specs=pl.BlockSpec((t, 2 * KV_W), lambda b: (b, 0)),
        out_shape=jax.ShapeDtypeStruct((rows, 2 * KV_W), BF16),
        compiler_params=_params(1),
        name="fox_prep",
    )(lft, kb)


KEY_NEG_INF = INT_MIN + 0x7FFFFF


def _threshold(key):
    bits = key ^ ((key >> 31) & 0x7FFFFFFF)
    return jnp.where(key <= KEY_NEG_INF, -jnp.inf, pltpu.bitcast(bits, F32))


def _tree_reduce(parts, fn):
    while len(parts) > 1:
        parts = [fn(parts[j], parts[j + 1]) for j in range(0, len(parts) - 1, 2)] + (
            parts[-1:] if len(parts) % 2 else [])
    return parts[0]


def _kth_largest(count_ge, k, n_all, bits_per_step=1):
    kf = float(k)
    zero_key = jnp.zeros(n_all.shape, I32)
    c0 = count_ge(_threshold(zero_key))
    nonneg = c0 >= kf
    state = (jnp.where(nonneg, zero_key, INT_MIN), jnp.where(nonneg, c0, n_all))

    def step(state, shift, n_bits):
        prefix, n_ge = state
        new_prefix, new_ge = prefix, n_ge
        for j in range(1, 2 ** n_bits):
            cand = prefix + lax.shift_left(jnp.int32(j), shift)
            c = count_ge(_threshold(cand))
            keep = c >= kf
            new_prefix, new_ge = jnp.where(keep, cand, new_prefix), jnp.where(keep, c, new_ge)
        return new_prefix, new_ge

    low_bits = 31
    n_steps, rest = divmod(low_bits, bits_per_step)
    state = lax.fori_loop(
        0, n_steps, lambda it, st: step(st, low_bits - bits_per_step * (it + 1), bits_per_step), state)
    if rest:
        state = step(state, 0, rest)
    return _threshold(state[0]), state[1]


def _tie_cut(count_tie_below, need, n_bits):
    def body(it, j):
        cand = j + lax.shift_left(jnp.int32(1), n_bits - 1 - it)
        return jnp.where(count_tie_below(cand) < need, cand, j)

    return lax.fori_loop(0, n_bits, body, jnp.zeros(need.shape, I32))


def _stage_major(n_items, stages):
    for stage in stages:
        for item in range(n_items):
            stage(item)


def _attn_scratch(tq, ck, q_width):
    cols = GROUP * tq
    return [pltpu.VMEM((N_KV_HEADS, cols, q_width), BF16), pltpu.VMEM((2, N_KV_HEADS, ck, cols), F32),
            pltpu.VMEM((N_KV_HEADS, 1, cols), F32), pltpu.VMEM((N_KV_HEADS, 1, cols), F32),
            pltpu.VMEM((N_KV_HEADS, LANES, cols), F32)]


def _attn_init(m_scr, l_scr, acc_scr):
    m_scr[...] = jnp.full(m_scr.shape, NEG, F32)
    l_scr[...] = jnp.zeros(l_scr.shape, F32)
    acc_scr[...] = jnp.zeros(acc_scr.shape, F32)


def _attn_chunks(n_ck, raw_logits, finish_logits, vt_fn, s_scr, m_scr, l_scr, acc_scr):
    def produce(c, slot):
        for kv in range(N_KV_HEADS):
            s_scr[slot, kv] = raw_logits(c, kv)

    def consume(c, slot, last):
        p_all, a_all = {}, {}

        def softmax(kv):
            s = finish_logits(c, s_scr[slot, kv], last)
            m_old = m_scr[kv]
            mn = jnp.maximum(m_old, jnp.max(s, axis=0, keepdims=True))
            a_all[kv] = jnp.exp2(m_old - mn)
            p = jnp.exp2(s - mn)
            l_scr[kv] = a_all[kv] * l_scr[kv] + jnp.sum(p, axis=0, keepdims=True)
            m_scr[kv] = mn
            p_all[kv] = p.astype(BF16)

        def weighted_values(kv):
            acc_scr[kv] = a_all.pop(kv) * acc_scr[kv] + _dot(vt_fn(c, kv), p_all.pop(kv))

        _stage_major(N_KV_HEADS, (softmax, weighted_values))

    produce(0, 0)
    n_pairs = (n_ck - 1) // 2

    def body(j, carry):
        c = 2 * j
        produce(c + 1, 1)
        consume(c, 0, False)
        produce(c + 2, 0)
        consume(c + 1, 1, False)
        return carry

    lax.fori_loop(0, n_pairs, body, 0)
    c = 2 * n_pairs
    left = n_ck - 1 - c

    @pl.when(left == 0)
    def _():
        consume(c, 0, True)

    @pl.when(left == 1)
    def _():
        produce(c + 1, 1)
        consume(c, 0, False)
        consume(c + 1, 1, True)


def _attn_finish(o_ref, gz_ref, l_scr, acc_scr, tq):
    for kv in range(N_KV_HEADS):
        o = (acc_scr[kv] / l_scr[kv]).T
        for g in range(GROUP):
            c0 = (kv * GROUP + g) * LANES
            o_ref[:, c0:c0 + LANES] = (o[g * tq:(g + 1) * tq] * gz_ref[:, c0:c0 + LANES].astype(F32)).astype(BF16)


def _dsa_prompt_kernel(q_ref, qi_ref, qis_ref, wit_ref, gz_ref, k_ref, vt_ref, kia_ref, o_ref, s_scr, b_scr, qi2_scr,
                       q2_scr, ls_scr, m_scr, l_scr, acc_scr, *, tq, ck, topk, n_bits):
    i = pl.program_id(1)
    n_ck = ((i + 1) * tq + ck - 1) // ck
    kpos = lax.broadcasted_iota(I32, (ck, tq), 0)
    qpos = i * tq + lax.broadcasted_iota(I32, (ck, tq), 1)
    wit = wit_ref[...]

    n_pairs = IDX_HEADS // 2
    for p in range(n_pairs):
        qi2_scr[p, 0:tq, :] = qi_ref[:, p * LANES:(p + 1) * LANES]
        qi2_scr[p, tq:2 * tq, :] = qis_ref[:, p * LANES:(p + 1) * LANES]

    def score_chunk(c, carry):
        ka = kia_ref[pl.ds(c * ck, ck), :]
        dots = [_dot_nt(ka, qi2_scr[p]) for p in range(n_pairs)]
        acc = None
        for hh in range(IDX_HEADS):
            d = dots[hh // 2][:, (hh % 2) * tq:(hh % 2 + 1) * tq]
            term = jnp.maximum(d, 0.0) * wit[hh:hh + 1, :]
            acc = term if acc is None else acc + term
        s_scr[c] = jnp.where(c * ck + kpos <= qpos, acc, -jnp.inf)
        return carry

    lax.fori_loop(0, n_ck, score_chunk, 0)

    def count(pred):
        def body(c, acc):
            m = jnp.where(pred(s_scr[c], c * ck + kpos), 1.0, 0.0)
            return acc + _tree_reduce([m[r:r + SUBLANES] for r in range(0, ck, SUBLANES)], jnp.add)

        acc = lax.fori_loop(0, n_ck, body, jnp.zeros((SUBLANES, tq), F32))
        return jnp.sum(acc, axis=0, keepdims=True)

    n_all = jnp.zeros((1, tq), F32) + (n_ck * ck).astype(F32)
    v, n_ge = _kth_largest(lambda cand: count(lambda s, pos: s >= cand), topk, n_all)
    has_tie = jnp.max(jnp.where((n_ge > float(topk)) & (v > -jnp.inf), 1.0, 0.0)) > 0.5

    def tie_cut():
        need = float(topk) - count(lambda s, pos: s > v)
        return _tie_cut(lambda cand: count(lambda s, pos: (s == v) & (pos < cand)), need, n_bits)

    jcut = lax.cond(has_tie, tie_cut, lambda: jnp.full((1, tq), 2 ** n_bits, I32))

    def bias_chunk(c, carry):
        s = s_scr[c]
        sel = ((s > v) | ((s == v) & (c * ck + kpos <= jcut))) & (s > -jnp.inf)
        b_scr[c] = jnp.where(sel, 0.0, NEG)
        return carry

    lax.fori_loop(0, n_ck, bias_chunk, 0)

    for kv in range(N_KV_HEADS):
        for g in range(GROUP):
            h = kv * GROUP + g
            q2_scr[kv, g * tq:(g + 1) * tq, :] = q_ref[:, h * LANES:(h + 1) * LANES]
    _attn_init(m_scr, l_scr, acc_scr)

    _attn_chunks(
        n_ck,
        lambda c, kv: _dot_nt(k_ref[pl.ds(c * ck, ck), kv * LANES:(kv + 1) * LANES], q2_scr[kv]),
        lambda c, s, last: s + jnp.concatenate([b_scr[c]] * GROUP, axis=1),
        lambda c, kv: vt_ref[c, kv * LANES:(kv + 1) * LANES, :],
        ls_scr, m_scr, l_scr, acc_scr)
    _attn_finish(o_ref, gz_ref, l_scr, acc_scr, tq)


def _dsa_prompt_call(q, qi, qis, wit, gz, kb, vt, kia, *, n_batch):
    rows = q.shape[0]
    t = rows // n_batch
    tq = min(128, t)
    ck = vt.shape[2]
    n_c = t // ck
    n_q = t // tq
    topk = min(TOPK_MAX, t // 4)
    n_bits = max(1, int(math.ceil(math.log2(t))))
    qrow = lambda w: pl.BlockSpec((tq, w), lambda b, i: (b * n_q + i, 0))
    seq = lambda w: pl.BlockSpec((t, w), lambda b, i: (b, 0))
    return pl.pallas_call(
        functools.partial(_dsa_prompt_kernel, tq=tq, ck=ck, topk=topk, n_bits=n_bits),
        grid=(n_batch, n_q),
        in_specs=[qrow(ATT_W), qrow(IDX_HEADS * IDX_DIM), qrow(IDX_HEADS * IDX_DIM),
                  pl.BlockSpec((IDX_HEADS, tq), lambda b, i: (0, b * n_q + i)),
                  qrow(ATT_W), seq(KV_W), pl.BlockSpec((n_c, KV_W, ck), lambda b, i: (b, 0, 0)), seq(LANES)],
        out_specs=qrow(ATT_W),
        out_shape=jax.ShapeDtypeStruct((rows, ATT_W), BF16),
        scratch_shapes=[pltpu.VMEM((n_c, ck, tq), F32), pltpu.VMEM((n_c, ck, tq), F32),
                        pltpu.VMEM((IDX_HEADS // 2, GROUP * tq, LANES), BF16)] + _attn_scratch(tq, ck, LANES),
        compiler_params=_params(2),
        name="dsa_prompt_attn",
    )(q, qi, qis, wit, gz, kb, vt, kia)


def _fox_prompt_kernel(q_ref, gz_ref, kx_ref, vt_ref, o_ref, q2_scr, ls_scr, m_scr, l_scr, acc_scr, *, tq, ck):
    i = pl.program_id(1)
    n_ck = ((i + 1) * tq + ck - 1) // ck
    lane = lax.broadcasted_iota(I32, (tq, LANES), 1)
    for kv in range(N_KV_HEADS):
        for g in range(GROUP):
            h = kv * GROUP + g
            ones = jnp.where(((lane & (N_HEADS - 1)) == h) & (lane < BIAS_PARTS * N_HEADS), 1.0, 0.0)
            q2_scr[kv, g * tq:(g + 1) * tq, 0:LANES] = q_ref[:, h * LANES:(h + 1) * LANES]
            q2_scr[kv, g * tq:(g + 1) * tq, LANES:2 * LANES] = ones.astype(BF16)
    _attn_init(m_scr, l_scr, acc_scr)

    def causal_mask(c, s, last):
        if not last:
            return s
        key = c * ck + lax.broadcasted_iota(I32, (ck, GROUP * tq), 0)
        qpos = i * tq + (lax.broadcasted_iota(I32, (ck, GROUP * tq), 1) & (tq - 1))
        return jnp.where(key <= qpos, s, NEG)

    _attn_chunks(
        n_ck,
        lambda c, kv: _dot_nt(kx_ref[pl.ds(c * ck, ck), 2 * kv * LANES:(2 * kv + 2) * LANES], q2_scr[kv]),
        causal_mask,
        lambda c, kv: vt_ref[c, kv * LANES:(kv + 1) * LANES, :],
        ls_scr, m_scr, l_scr, acc_scr)
    _attn_finish(o_ref, gz_ref, l_scr, acc_scr, tq)


def _fox_prompt_call(q, gz, kx, vt, *, n_batch, tq):
    rows = q.shape[0]
    t = rows // n_batch
    ck = vt.shape[2]
    n_c = t // ck
    n_q = t // tq
    assert tq & (tq - 1) == 0 and BIAS_PARTS * N_HEADS <= LANES
    qrow = lambda w: pl.BlockSpec((tq, w), lambda b, i: (b * n_q + i, 0))
    return pl.pallas_call(
        functools.partial(_fox_prompt_kernel, tq=tq, ck=ck),
        grid=(n_batch, n_q),
        in_specs=[qrow(ATT_W), qrow(ATT_W), pl.BlockSpec((t, 2 * KV_W), lambda b, i: (b, 0)),
                  pl.BlockSpec((n_c, KV_W, ck), lambda b, i: (b, 0, 0))],
        scratch_shapes=_attn_scratch(tq, ck, 2 * LANES),
        out_specs=qrow(ATT_W),
        out_shape=jax.ShapeDtypeStruct((rows, ATT_W), BF16),
        compiler_params=_params(2),
        name="fox_prompt_attn",
    )(q, gz, kx, vt)


def _group_rows(x0, x1, row):
    return jnp.where(row < SAMPLE_ROWS // 2, x0, pltpu.roll(x1, SAMPLE_ROWS // 2, 0))


def _page_block(new_ref):
    new = new_ref[0]
    return jnp.concatenate([new, jnp.zeros((PAGE_SIZE - SAMPLE_ROWS, new.shape[1]), new.dtype)], axis=0)


def _kv_pages(page_refs, new_ref, kv):
    pages = [r[0, 0, pl.ds(kv, PAGE_SIZE, stride=N_KV_HEADS), :].astype(BF16) for r in page_refs]
    new = new_ref[0, pl.ds(kv, SAMPLE_ROWS, stride=N_KV_HEADS), :]
    new = jnp.concatenate([new, jnp.zeros((PAGE_SIZE - SAMPLE_ROWS, LANES), new.dtype)], axis=0)
    return pages + [new.astype(BF16)]


def _sample_attend(q_ref, gz_ref, k_refs, kn_ref, v_refs, vn_ref, bias_fn, o_ref):
    row = lax.broadcasted_iota(I32, (SAMPLE_ROWS, LANES), 0)
    real = row < SAMPLE_ROWS // 2
    qf = q_ref[0].astype(F32)
    gz = gz_ref[0].astype(F32)
    for kv in range(N_KV_HEADS):
        h0 = kv * GROUP
        sl = lambda a, h: a[:, h * LANES:(h + 1) * LANES]
        q2 = _group_rows(sl(qf, h0), sl(qf, h0 + 1), row).astype(BF16)
        s = jnp.concatenate([_dot_nt(q2, kp) for kp in _kv_pages(k_refs, kn_ref, kv)], axis=1)
        s = s + bias_fn(kv)
        m = jnp.max(s, axis=1, keepdims=True)
        p = jnp.exp2(s - m)
        l = jnp.sum(p, axis=1, keepdims=True)
        pb = p.astype(BF16)
        o = None
        for n, vp in enumerate(_kv_pages(v_refs, vn_ref, kv)):
            t = _dot(pb[:, n * LANES:(n + 1) * LANES], vp)
            o = t if o is None else o + t
        o = o / l
        o_ref[0, :, h0 * LANES:(h0 + 1) * LANES] = jnp.where(real, o * sl(gz, h0), 0.0).astype(BF16)
        o_ref[0, :, (h0 + 1) * LANES:(h0 + 2) * LANES] = jnp.where(
            real, pltpu.roll(o, SAMPLE_ROWS // 2, 0) * sl(gz, h0 + 1), 0.0).astype(BF16)


def _dsa_sample_kernel(pt_ref, q_ref, qi_ref, wi_ref, gz_ref, kn_ref, vn_ref, kin_ref, *rest,
                       n_pages, topk, n_bits):
    k_refs, v_refs, ki_refs = rest[:n_pages], rest[n_pages:2 * n_pages], rest[2 * n_pages:3 * n_pages]
    o_ref = rest[3 * n_pages]
    n_chunks = n_pages + 1
    n_keys = n_chunks * PAGE_SIZE
    past = n_pages * PAGE_SIZE
    qi = qi_ref[0]
    wi = wi_ref[0]

    def score(dots):
        t = jnp.maximum(dots, 0.0) * wi
        return jnp.sum(t.reshape(SAMPLE_ROWS, IDX_HEADS, PAGE_SIZE), axis=1)

    chunks = [score(_dot(qi, r[0, 0].astype(BF16))) for r in ki_refs]
    chunks.append(score(_dot_nt(qi, _page_block(kin_ref).astype(BF16))))
    score_all = jnp.concatenate(chunks, axis=1)
    row = lax.broadcasted_iota(I32, (SAMPLE_ROWS, n_keys), 0)
    col = lax.broadcasted_iota(I32, (SAMPLE_ROWS, n_keys), 1)
    causal = col <= past + row
    s = jnp.where(causal, score_all, -jnp.inf)

    count = lambda m: jnp.sum(jnp.where(m, 1.0, 0.0), axis=1, keepdims=True)
    v, n_ge = _kth_largest(lambda cand: count(s >= cand), topk, jnp.full((SAMPLE_ROWS, 1), float(n_keys), F32),
                           bits_per_step=2)
    has_tie = jnp.max(jnp.where((n_ge > float(topk)) & (v > -jnp.inf), 1.0, 0.0)) > 0.5

    def tie_cut():
        need = float(topk) - count(s > v)
        return _tie_cut(lambda cand: count((s == v) & (col < cand)), need, n_bits)

    jcut = lax.cond(has_tie, tie_cut, lambda: jnp.full((SAMPLE_ROWS, 1), 2 ** n_bits, I32))
    sel = ((s > v) | ((s == v) & (col <= jcut))) & causal
    bias = jnp.where(sel, 0.0, NEG)
    bias2 = jnp.where(row < SAMPLE_ROWS // 2, bias, pltpu.roll(bias, SAMPLE_ROWS // 2, 0))

    _sample_attend(q_ref, gz_ref, k_refs, kn_ref, v_refs, vn_ref, lambda kv: bias2, o_ref)


def _fox_sample_kernel(pt_ref, q_ref, gz_ref, kn_ref, vn_ref, lfn_ref, *rest, n_pages):
    k_refs, v_refs, lf_refs = rest[:n_pages], rest[n_pages:2 * n_pages], rest[2 * n_pages:3 * n_pages]
    o_ref = rest[3 * n_pages]
    n_keys = (n_pages + 1) * PAGE_SIZE
    past = n_pages * PAGE_SIZE
    lf_t = [r[0, 0] for r in lf_refs]
    c_all = _lane_cumsum(jnp.concatenate(lf_t + [lfn_ref[0]], axis=1)) * LOG2E
    row = lax.broadcasted_iota(I32, (SAMPLE_ROWS, n_keys), 0)
    col = lax.broadcasted_iota(I32, (SAMPLE_ROWS, n_keys), 1)
    tok = row & (SAMPLE_ROWS // 2 - 1)
    mask = jnp.where(col <= past + tok, 0.0, NEG)

    def bias_fn(kv):
        h0 = kv * GROUP
        c2 = jnp.where(row < SAMPLE_ROWS // 2, c_all[h0:h0 + 1, :], c_all[h0 + 1:h0 + 2, :])
        return mask - c2

    _sample_attend(q_ref, gz_ref, k_refs, kn_ref, v_refs, vn_ref, bias_fn, o_ref)


def _page_specs(layer, n_pages, rows, width):
    def spec(p):
        return pl.BlockSpec((1, 1, rows, width), lambda b, pt: (layer, pt[b * n_pages + p], 0, 0))
    return [spec(p) for p in range(n_pages)]


def _kv_page_specs(layer, n_pages):
    return _page_specs(layer, n_pages, PAGE_SIZE * N_KV_HEADS, HEAD_DIM)


def _seq_spec(a):
    return pl.BlockSpec((1,) + a.shape[1:], lambda b, pt: (b,) + (0,) * (a.ndim - 1))


def _dsa_sample_call(pt_flat, q, qi, wi, gz, kn, vn, kin, cache_k, cache_v, cache_ki, *, layer, n_pages, dec_seq):
    n_seq = q.shape[0]
    n_keys = (n_pages + 1) * PAGE_SIZE
    topk = min(TOPK_MAX, (n_pages * PAGE_SIZE + dec_seq) // 4)
    n_bits = int(math.ceil(math.log2(n_keys)))
    seq_in = [q, qi, wi, gz, kn, vn, kin]
    grid_spec = pltpu.PrefetchScalarGridSpec(
        num_scalar_prefetch=1,
        grid=(n_seq,),
        in_specs=[_seq_spec(a) for a in seq_in]
        + _kv_page_specs(layer, n_pages) + _kv_page_specs(layer, n_pages)
        + _page_specs(layer, n_pages, IDX_DIM, PAGE_SIZE),
        out_specs=pl.BlockSpec((1, SAMPLE_ROWS, ATT_W), lambda b, pt: (b, 0, 0)),
    )
    return pl.pallas_call(
        functools.partial(_dsa_sample_kernel, n_pages=n_pages, topk=topk, n_bits=n_bits),
        grid_spec=grid_spec,
        out_shape=jax.ShapeDtypeStruct((n_seq, SAMPLE_ROWS, ATT_W), BF16),
        compiler_params=_params(1),
        name="dsa_sample_attn",
    )(pt_flat, *seq_in, *([cache_k] * n_pages), *([cache_v] * n_pages), *([cache_ki] * n_pages))


def _fox_sample_call(pt_flat, q, gz, kn, vn, lfn, cache_k, cache_v, cache_lf, *, layer, n_pages):
    n_seq = q.shape[0]
    seq_in = [q, gz, kn, vn, lfn]
    grid_spec = pltpu.PrefetchScalarGridSpec(
        num_scalar_prefetch=1,
        grid=(n_seq,),
        in_specs=[_seq_spec(a) for a in seq_in]
        + _kv_page_specs(layer, n_pages) + _kv_page_specs(layer, n_pages)
        + _page_specs(layer, n_pages, N_HEADS, PAGE_SIZE),
        out_specs=pl.BlockSpec((1, SAMPLE_ROWS, ATT_W), lambda b, pt: (b, 0, 0)),
    )
    return pl.pallas_call(
        functools.partial(_fox_sample_kernel, n_pages=n_pages),
        grid_spec=grid_spec,
        out_shape=jax.ShapeDtypeStruct((n_seq, SAMPLE_ROWS, ATT_W), BF16),
        compiler_params=_params(1),
        name="fox_sample_attn",
    )(pt_flat, *seq_in, *([cache_k] * n_pages), *([cache_v] * n_pages), *([cache_lf] * n_pages))


def _rope_tables(pos, dim, rot_dim):
    half = rot_dim // 2
    inv = ROPE_THETA ** (-jnp.arange(half, dtype=F32) / half)
    ang = pos.astype(F32)[:, None] * inv[None, :]
    cos, sin = jnp.cos(ang), jnp.sin(ang)
    n = pos.shape[0]
    cos_t = jnp.concatenate([cos, cos, jnp.ones((n, dim - rot_dim), F32)], axis=1)
    sin_t = jnp.concatenate([-sin, sin, jnp.zeros((n, dim - rot_dim), F32)], axis=1)
    rep = LANES // dim
    return jnp.tile(cos_t, (1, rep)), jnp.tile(sin_t, (1, rep))


def kernel(x_prompt, x_sample, cache_dsa_k, cache_dsa_v, cache_dsa_kidx, cache_fox_k, cache_fox_v, cache_fox_logf,
           page_table, c_prompt, c_sample, norm_g, w_ada, b_ada, w_in_dsa, w_out_dsa, w_in_fox, b_forget,
           w_out_fox, final_g):
    n_b, t_p, d = x_prompt.shape
    n_s, t_s, _ = x_sample.shape
    depth = norm_g.shape[0]
    n_pages = page_table.shape[1]
    past = n_pages * PAGE_SIZE
    n_pool = cache_dsa_k.shape[1]
    assert t_s <= SAMPLE_ROWS // 2 and d == ATT_W

    n_c = n_b + n_s
    n_cp = -(-n_c // SUBLANES) * SUBLANES
    c_all = jnp.concatenate([c_prompt, c_sample, jnp.zeros((n_cp - n_c, d), F32)], axis=0)
    mod = _ada_call(c_all, w_ada, b_ada)

    pos_p = jnp.arange(t_p)
    pos_s = jnp.tile(past + jnp.arange(SAMPLE_ROWS), n_s)
    tabs_p = _rope_tables(pos_p, HEAD_DIM, ROT_DIM) + _rope_tables(pos_p, IDX_DIM, IDX_ROT_DIM)
    tabs_s = _rope_tables(pos_s, HEAD_DIM, ROT_DIM) + _rope_tables(pos_s, IDX_DIM, IDX_ROT_DIM)

    rows_p = n_b * t_p
    rows_s = n_s * SAMPLE_ROWS
    tm_p = _row_tile(t_p)
    tpb = t_p // tm_p
    tm_s = _row_tile(rows_s)
    xp = x_prompt.reshape(rows_p, d)
    xs = jnp.pad(x_sample, ((0, 0), (0, SAMPLE_ROWS - t_s), (0, 0))).reshape(rows_s, d)
    pt_flat = page_table.reshape(-1).astype(I32)

    page_view = lambda c: c.reshape(c.shape[0], n_pool, PAGE_SIZE * N_KV_HEADS, HEAD_DIM)
    ck_dsa, cv_dsa, ck_fox, cv_fox = (page_view(c) for c in (cache_dsa_k, cache_dsa_v, cache_fox_k, cache_fox_v))
    cki_dsa = jnp.swapaxes(cache_dsa_kidx, 2, 3)
    clf_fox = jnp.swapaxes(cache_fox_logf, 2, 3)

    fg = final_g.reshape(1, d)
    tq_fox = min(128, t_p)
    outs = {k: [] for k in ("dk_p", "dv_p", "dki_p", "fk_p", "fv_p", "fl_p", "dk_s", "dv_s", "dki_s", "fk_s", "fv_s", "fl_s")}

    h3 = lambda a: a.reshape(n_s, SAMPLE_ROWS * N_KV_HEADS, HEAD_DIM)
    h4 = lambda a: a.reshape(n_s, SAMPLE_ROWS, N_KV_HEADS, HEAD_DIM)

    def per_row(a):
        return jnp.repeat(a, SAMPLE_ROWS, axis=0).reshape(rows_s // tm_s, tm_s, d)

    for i in range(depth):
        j = i // N_MIXERS
        g = norm_g[i].reshape(1, d)
        shift_p, scale_p, gate_p = (mod[i, :n_b, k * d:(k + 1) * d].reshape(n_b, 1, d) for k in range(3))
        shift_s, scale_s, gate_s = (per_row(mod[i, n_b:n_c, k * d:(k + 1) * d]) for k in range(3))
        final = i == depth - 1
        if i % N_MIXERS == 0:
            w = w_in_dsa[j]
            n_main = 2 * ATT_W + 2 * KV_W + IDX_HEADS * IDX_DIM
            w_main = w[:, :n_main].astype(BF16)
            w_tail = jnp.pad(w[:, n_main:], ((0, 0), (0, LANES - (w.shape[1] - n_main)))).astype(BF16)
            w_vt = w[:, ATT_W + KV_W:ATT_W + 2 * KV_W].T.astype(BF16)
            w_wit = jnp.pad(w[:, n_main + IDX_DIM:].T, ((0, 2 * SUBLANES - IDX_HEADS), (0, 0))).astype(BF16)
            w_out = w_out_dsa[j].astype(BF16)
            q, kf, kb, vf, vt, gz, qi, qis, kif, kia, wi, wit = _inproj_dsa_call(
                xp, scale_p, shift_p, g, tabs_p, w_main, w_tail, w_vt, w_wit, tiles_per_batch=tpb, per_row=False)
            og = _dsa_prompt_call(q, qi, qis, wit, gz, kb, vt, kia, n_batch=n_b)
            xp = _outproj_call(og, xp, gate_p, w_out, fg, tiles_per_batch=tpb, per_row=False, final=final)
            outs["dk_p"].append(kf.reshape(n_b, t_p, N_KV_HEADS, HEAD_DIM))
            outs["dv_p"].append(vf.reshape(n_b, t_p, N_KV_HEADS, HEAD_DIM))
            outs["dki_p"].append(kif.reshape(n_b, t_p, IDX_DIM))
            q, kf, kb, vf, vt, gz, qi, qis, kif, kia, wi, wit = _inproj_dsa_call(
                xs, scale_s, shift_s, g, tabs_s, w_main, w_tail, w_vt, w_wit, tiles_per_batch=1, per_row=True)
            r3 = lambda a: a.reshape(n_s, SAMPLE_ROWS, a.shape[-1])
            og = _dsa_sample_call(
                pt_flat, r3(q), qi.reshape(n_s, SAMPLE_ROWS * IDX_HEADS, IDX_DIM),
                wi.reshape(n_s, SAMPLE_ROWS * IDX_HEADS, 1), r3(gz), h3(kf), h3(vf), r3(kif),
                ck_dsa, cv_dsa, cki_dsa, layer=j, n_pages=n_pages, dec_seq=t_s)
            xs = _outproj_call(og.reshape(rows_s, ATT_W), xs, gate_s, w_out, fg, tiles_per_batch=1, per_row=True,
                               final=final)
            outs["dk_s"].append(h4(kf)[:, :t_s])
            outs["dv_s"].append(h4(vf)[:, :t_s])
            outs["dki_s"].append(r3(kif)[:, :t_s])
        else:
            w = w_in_fox[j]
            n_main = 2 * ATT_W + 2 * KV_W
            w_main = w[:, :n_main].astype(BF16)
            w_vt = w[:, ATT_W + KV_W:ATT_W + 2 * KV_W].T.astype(BF16)
            w_flt = jnp.pad(w[:, n_main:].T, ((0, 2 * SUBLANES - N_HEADS), (0, 0))).astype(BF16)
            b_f = b_forget[j].reshape(N_HEADS, 1)
            w_out = w_out_fox[j].astype(BF16)
            q, kf, kb, vf, vt, gz, lft = _inproj_fox_call(
                xp, scale_p, shift_p, g, w_main, w_vt, w_flt, b_f, tiles_per_batch=tpb, per_row=False)
            kx = _fox_prep_call(lft, kb, n_b)
            og = _fox_prompt_call(q, gz, kx, vt, n_batch=n_b, tq=tq_fox)
            xp = _outproj_call(og, xp, gate_p, w_out, fg, tiles_per_batch=tpb, per_row=False, final=final)
            outs["fk_p"].append(kf.reshape(n_b, t_p, N_KV_HEADS, HEAD_DIM))
            outs["fv_p"].append(vf.reshape(n_b, t_p, N_KV_HEADS, HEAD_DIM))
            outs["fl_p"].append(lft.T.reshape(n_b, t_p, N_HEADS))
            q, kf, kb, vf, vt, gz, lft = _inproj_fox_call(
                xs, scale_s, shift_s, g, w_main, w_vt, w_flt, b_f, tiles_per_batch=1, per_row=True)
            r3 = lambda a: a.reshape(n_s, SAMPLE_ROWS, a.shape[-1])
            lf3 = lft.reshape(N_HEADS, n_s, SAMPLE_ROWS).transpose(1, 0, 2)
            lfn = jnp.pad(lf3, ((0, 0), (0, 0), (0, PAGE_SIZE - SAMPLE_ROWS)))
            og = _fox_sample_call(pt_flat, r3(q), r3(gz), h3(kf), h3(vf), lfn, ck_fox, cv_fox, clf_fox,
                                  layer=j, n_pages=n_pages)
            xs = _outproj_call(og.reshape(rows_s, ATT_W), xs, gate_s, w_out, fg, tiles_per_batch=1, per_row=True,
                               final=final)
            outs["fk_s"].append(h4(kf)[:, :t_s])
            outs["fv_s"].append(h4(vf)[:, :t_s])
            outs["fl_s"].append(lf3.transpose(0, 2, 1)[:, :t_s])

    y_prompt = xp.reshape(n_b, t_p, d)
    y_sample = xs.reshape(n_s, SAMPLE_ROWS, d)[:, :t_s]
    st = lambda k: jnp.stack(outs[k])
    return (y_prompt, y_sample, st("dk_p"), st("dv_p"), st("dki_p"), st("fk_p"), st("fv_p"), st("fl_p"),
            st("dk_s"), st("dv_s"), st("dki_s"), st("fk_s"), st("fv_s"), st("fl_s"))
```

```python
import functools
import math

import jax
import jax.numpy as jnp
import numpy as np
from jax import lax
from jax.experimental import pallas as pl
from jax.experimental.pallas import tpu as pltpu

N_HEADS = 8
HEAD_DIM = 128
N_KV_HEADS = 4
GROUP = N_HEADS // N_KV_HEADS
ROT_DIM = HEAD_DIM // 4
ROPE_THETA = 500000.0
IDX_HEADS = 8
IDX_DIM = 64
IDX_ROT_DIM = IDX_DIM // 4
TOPK_MAX = 256
PAGE_SIZE = 128
EPS = 1e-6
ATT_W = N_HEADS * HEAD_DIM
KV_W = N_KV_HEADS * HEAD_DIM
N_MIXERS = 2

LANES = 128
SUBLANES = 8
VMEM_LIMIT = 56 * 1024 * 1024

SAMPLE_ROWS = SUBLANES
NEG = -1e30
LOG2E = math.log2(math.e)
INT_MIN = -(2 ** 31)

F32 = jnp.float32
BF16 = jnp.bfloat16
I32 = jnp.int32


def _params(n_axes):
    return pltpu.CompilerParams(dimension_semantics=("arbitrary",) * n_axes, vmem_limit_bytes=VMEM_LIMIT)


def _dot(a, b):
    return jnp.dot(a, b, preferred_element_type=F32)


def _dot_nt(a, b):
    return lax.dot_general(a, b, (((1,), (1,)), ((), ())), preferred_element_type=F32)


def _sigmoid(x):
    return 1.0 / (1.0 + jnp.exp(-x))


def _ada_kernel(c_ref, w_ref, b_ref, o_ref):
    c = c_ref[...].astype(BF16)
    w = w_ref[0].astype(BF16)
    o_ref[0] = _dot(c, w) + b_ref[0]


def _ada_call(c_all, w_ada, b_ada):
    depth, d, n3 = w_ada.shape
    m = c_all.shape[0]
    tn = d
    return pl.pallas_call(
        _ada_kernel,
        grid=(depth, n3 // tn),
        in_specs=[
            pl.BlockSpec((m, d), lambda l, j: (0, 0)),
            pl.BlockSpec((1, d, tn), lambda l, j: (l, 0, j)),
            pl.BlockSpec((1, 1, tn), lambda l, j: (l, 0, j)),
        ],
        out_specs=pl.BlockSpec((1, m, tn), lambda l, j: (l, 0, j)),
        out_shape=jax.ShapeDtypeStruct((depth, m, n3), F32),
        compiler_params=_params(2),
        name="ada_mod",
    )(c_all, w_ada, b_ada.reshape(depth, 1, n3))


def _modulated_input(x_ref, sc_ref, sh_ref, g_ref):
    x = x_ref[...]
    ms = jnp.mean(x * x, axis=-1, keepdims=True)
    y = x * lax.rsqrt(ms + EPS) * g_ref[...]
    return (y * (1.0 + sc_ref[0]) + sh_ref[0]).astype(BF16)


def _store_head(ref, head, value):
    ref[pl.ds(head, value.shape[0], stride=N_KV_HEADS), :] = value


def _head_rows_spec(tm):
    return pl.BlockSpec((tm * N_KV_HEADS, HEAD_DIM), lambda i: (i, 0))


def _rope(v, lane, cos_t, sin_t, period, half):
    vr = jnp.where((lane & (period - 1)) < half, pltpu.roll(v, LANES - half, 1), pltpu.roll(v, half, 1))
    return v * cos_t + vr * sin_t


def _inproj_dsa_kernel(x_ref, sc_ref, sh_ref, g_ref, c128_ref, s128_ref, c64_ref, s64_ref, w_ref, wt_ref,
                       wvt_ref, wwt_ref,
                       q_ref, kf_ref, kb_ref, vf_ref, vt_ref, gz_ref, qi_ref, qis_ref, kif_ref, kia_ref, wi_ref,
                       wit_ref, *, q_scale, wi_scale):
    h = _modulated_input(x_ref, sc_ref, sh_ref, g_ref)
    tm = h.shape[0]
    lane = lax.broadcasted_iota(I32, (tm, LANES), 1)
    c128, s128, c64, s64 = c128_ref[...], s128_ref[...], c64_ref[...], s64_ref[...]
    rope_h = lambda v: _rope(v, lane, c128, s128, HEAD_DIM, ROT_DIM // 2)
    rope_i = lambda v: _rope(v, lane, c64, s64, IDX_DIM, IDX_ROT_DIM // 2)
    seg = 4 * LANES
    off = 0
    for s in range(ATT_W // seg):
        r = _dot(h, w_ref[:, off:off + seg])
        for j in range(seg // LANES):
            c0 = s * seg + j * LANES
            q_ref[:, c0:c0 + LANES] = (rope_h(r[:, j * LANES:(j + 1) * LANES]) * q_scale).astype(BF16)
        off += seg
    r = _dot(h, w_ref[:, off:off + KV_W])
    for j in range(N_KV_HEADS):
        kr = rope_h(r[:, j * LANES:(j + 1) * LANES])
        _store_head(kf_ref, j, kr)
        kb_ref[:, j * LANES:(j + 1) * LANES] = kr.astype(BF16)
    off += KV_W
    r = _dot(h, w_ref[:, off:off + KV_W])
    for j in range(N_KV_HEADS):
        _store_head(vf_ref, j, r[:, j * LANES:(j + 1) * LANES])
    vt_ref[0] = _dot_nt(wvt_ref[...], h).astype(BF16)
    off += KV_W
    for s in range(ATT_W // seg):
        r = _dot(h, w_ref[:, off:off + seg])
        gz_ref[:, s * seg:(s + 1) * seg] = (r * _sigmoid(r)).astype(BF16)
        off += seg
    r = _dot(h, w_ref[:, off:off + IDX_HEADS * IDX_DIM])
    for j in range(IDX_HEADS * IDX_DIM // LANES):
        pair = rope_i(r[:, j * LANES:(j + 1) * LANES])
        qi_ref[:, j * LANES:(j + 1) * LANES] = pair.astype(BF16)
        qis_ref[:, j * LANES:(j + 1) * LANES] = pltpu.roll(pair, IDX_DIM, 1).astype(BF16)
    t = _dot(h, wt_ref[...])
    tr = rope_i(t)
    kif_ref[...] = tr[:, 0:IDX_DIM]
    kia_ref[...] = jnp.where(lane < IDX_DIM, tr, 0.0).astype(BF16)
    wi_ref[...] = t[:, IDX_DIM:IDX_DIM + IDX_HEADS] * wi_scale
    wit_ref[...] = _dot_nt(wwt_ref[...], h)[0:IDX_HEADS, :] * wi_scale


def _inproj_fox_kernel(x_ref, sc_ref, sh_ref, g_ref, w_ref, wvt_ref, wfl_ref, bf_ref,
                       q_ref, kf_ref, kb_ref, vf_ref, vt_ref, gz_ref, lft_ref, *, q_scale):
    h = _modulated_input(x_ref, sc_ref, sh_ref, g_ref)
    seg = 4 * LANES
    off = 0
    for s in range(ATT_W // seg):
        r = _dot(h, w_ref[:, off:off + seg])
        q_ref[:, s * seg:(s + 1) * seg] = (r * q_scale).astype(BF16)
        off += seg
    r = _dot(h, w_ref[:, off:off + KV_W])
    for j in range(N_KV_HEADS):
        _store_head(kf_ref, j, r[:, j * LANES:(j + 1) * LANES])
    kb_ref[...] = r.astype(BF16)
    off += KV_W
    r = _dot(h, w_ref[:, off:off + KV_W])
    for j in range(N_KV_HEADS):
        _store_head(vf_ref, j, r[:, j * LANES:(j + 1) * LANES])
    vt_ref[0] = _dot_nt(wvt_ref[...], h).astype(BF16)
    off += KV_W
    for s in range(ATT_W // seg):
        r = _dot(h, w_ref[:, off:off + seg])
        gz_ref[:, s * seg:(s + 1) * seg] = (r * _sigmoid(r)).astype(BF16)
        off += seg
    z = _dot_nt(wfl_ref[...], h)[0:N_HEADS, :] + bf_ref[...]
    lft_ref[...] = jnp.minimum(z, 0.0) - jnp.log1p(jnp.exp(-jnp.abs(z)))


def _row_tile(rows):
    return 512 if rows % 512 == 0 else rows


def _mod_specs(tm, d, tiles_per_batch, per_row):
    if per_row:
        return pl.BlockSpec((1, tm, d), lambda i: (i, 0, 0))
    return pl.BlockSpec((1, 1, d), lambda i: (i // tiles_per_batch, 0, 0))


def _inproj_dsa_call(x, scale, shift, g, tabs, w_main, w_tail, w_vt, w_wit, *, tiles_per_batch, per_row):
    rows, d = x.shape
    tm = _row_tile(rows) if per_row else rows // (scale.shape[0] * tiles_per_batch)
    n_t = tabs[0].shape[0] // tm
    row = lambda w: pl.BlockSpec((tm, w), lambda i: (i, 0))
    tab = pl.BlockSpec((tm, LANES), lambda i: (i % n_t, 0))
    mod = _mod_specs(tm, d, tiles_per_batch, per_row)
    full = lambda a: pl.BlockSpec(a.shape, lambda i: (0,) * a.ndim)
    sd = jax.ShapeDtypeStruct
    heads_f32 = (_head_rows_spec(tm), sd((rows * N_KV_HEADS, HEAD_DIM), F32))
    outs = [(row(ATT_W), sd((rows, ATT_W), BF16)), heads_f32, (row(KV_W), sd((rows, KV_W), BF16)), heads_f32,
            (pl.BlockSpec((1, KV_W, tm), lambda i: (i, 0, 0)), sd((rows // tm, KV_W, tm), BF16)),
            (row(ATT_W), sd((rows, ATT_W), BF16)),
            (row(IDX_HEADS * IDX_DIM), sd((rows, IDX_HEADS * IDX_DIM), BF16)),
            (row(IDX_HEADS * IDX_DIM), sd((rows, IDX_HEADS * IDX_DIM), BF16)), (row(IDX_DIM), sd((rows, IDX_DIM), F32)),
            (row(LANES), sd((rows, LANES), BF16)),
            (row(IDX_HEADS), sd((rows, IDX_HEADS), F32)),
            (pl.BlockSpec((IDX_HEADS, tm), lambda i: (0, i)), sd((IDX_HEADS, rows), F32))]
    return pl.pallas_call(
        functools.partial(_inproj_dsa_kernel, q_scale=HEAD_DIM ** -0.5 * LOG2E,
                          wi_scale=IDX_HEADS ** -0.5 * IDX_DIM ** -0.5),
        grid=(rows // tm,),
        in_specs=[row(d), mod, mod, full(g), tab, tab, tab, tab, full(w_main), full(w_tail), full(w_vt), full(w_wit)],
        out_specs=[s for s, _ in outs],
        out_shape=[t for _, t in outs],
        compiler_params=_params(1),
        name="inproj_dsa",
    )(x, scale, shift, g, *tabs, w_main, w_tail, w_vt, w_wit)


def _inproj_fox_call(x, scale, shift, g, w_main, w_vt, w_flt, b_f, *, tiles_per_batch, per_row):
    rows, d = x.shape
    tm = _row_tile(rows) if per_row else rows // (scale.shape[0] * tiles_per_batch)
    row = lambda w: pl.BlockSpec((tm, w), lambda i: (i, 0))
    mod = _mod_specs(tm, d, tiles_per_batch, per_row)
    full = lambda a: pl.BlockSpec(a.shape, lambda i: (0,) * a.ndim)
    sd = jax.ShapeDtypeStruct
    return pl.pallas_call(
        functools.partial(_inproj_fox_kernel, q_scale=HEAD_DIM ** -0.5 * LOG2E),
        grid=(rows // tm,),
        in_specs=[row(d), mod, mod, full(g), full(w_main), full(w_vt), full(w_flt), full(b_f)],
        out_specs=[row(ATT_W), _head_rows_spec(tm), row(KV_W), _head_rows_spec(tm),
                   pl.BlockSpec((1, KV_W, tm), lambda i: (i, 0, 0)), row(ATT_W),
                   pl.BlockSpec((N_HEADS, tm), lambda i: (0, i))],
        out_shape=[sd((rows, ATT_W), BF16), sd((rows * N_KV_HEADS, HEAD_DIM), F32), sd((rows, KV_W), BF16),
                   sd((rows * N_KV_HEADS, HEAD_DIM), F32),
                   sd((rows // tm, KV_W, tm), BF16), sd((rows, ATT_W), BF16), sd((N_HEADS, rows), F32)],
        compiler_params=_params(1),
        name="inproj_fox",
    )(x, scale, shift, g, w_main, w_vt, w_flt, b_f)


def _outproj_kernel(og_ref, x_ref, gate_ref, w_ref, fg_ref, o_ref, *, final):
    xn = x_ref[...] + gate_ref[0] * _dot(og_ref[...], w_ref[...])
    if final:
        ms = jnp.mean(xn * xn, axis=-1, keepdims=True)
        xn = xn * lax.rsqrt(ms + EPS) * fg_ref[...]
    o_ref[...] = xn


def _outproj_call(og, x, gate, w_out, final_g, *, tiles_per_batch, per_row, final):
    rows, d = x.shape
    tm = _row_tile(rows) if per_row else rows // (gate.shape[0] * tiles_per_batch)
    row = lambda w: pl.BlockSpec((tm, w), lambda i: (i, 0))
    full = lambda a: pl.BlockSpec(a.shape, lambda i: (0,) * a.ndim)
    return pl.pallas_call(
        functools.partial(_outproj_kernel, final=final),
        grid=(rows // tm,),
        in_specs=[row(ATT_W), row(d), _mod_specs(tm, d, tiles_per_batch, per_row), full(w_out), full(final_g)],
        out_specs=row(d),
        out_shape=jax.ShapeDtypeStruct((rows, d), F32),
        compiler_params=_params(1),
        name="outproj",
    )(og, x, gate, w_out, final_g)


def _lane_cumsum(x):
    n = x.shape[-1]
    lane = lax.broadcasted_iota(I32, x.shape, x.ndim - 1)
    s = 1
    while s < n:
        x = x + jnp.where(lane >= s, pltpu.roll(x, s, x.ndim - 1), 0.0)
        s *= 2
    return x


BIAS_PARTS = 3


def _fox_prep_kernel(lft_ref, kb_ref, kx_ref):
    t = lft_ref.shape[1]
    c = _lane_cumsum(lft_ref[...]) * (-LOG2E)
    ct = jnp.concatenate([c, jnp.zeros((LANES - N_HEADS, t), F32)], axis=0).T
    ext = jnp.zeros((t, LANES), F32)
    rest = ct
    for part in range(BIAS_PARTS):
        piece = rest.astype(BF16).astype(F32)
        rest = rest - piece
        ext = ext + (piece if part == 0 else pltpu.roll(piece, part * N_HEADS, 1))
    ext = ext.astype(BF16)
    for kv in range(N_KV_HEADS):
        kx_ref[:, 2 * kv * LANES:(2 * kv + 1) * LANES] = kb_ref[:, kv * LANES:(kv + 1) * LANES]
        kx_ref[:, (2 * kv + 1) * LANES:(2 * kv + 2) * LANES] = ext


def _fox_prep_call(lft, kb, n_batch):
    h, rows = lft.shape
    t = rows // n_batch
    return pl.pallas_call(
        _fox_prep_kernel,
        grid=(n_batch,),
        in_specs=[pl.BlockSpec((h, t), lambda b: (0, b)), pl.BlockSpec((t, KV_W), lambda b: (b, 0))],
        out_specs=pl.BlockSpec((t, 2 * KV_W), lambda b: (b, 0)),
        out_shape=jax.ShapeDtypeStruct((rows, 2 * KV_W), BF16),
        compiler_params=_params(1),
        name="fox_prep",
    )(lft, kb)


KEY_NEG_INF = INT_MIN + 0x7FFFFF


def _threshold(key):
    bits = key ^ ((key >> 31) & 0x7FFFFFFF)
    return jnp.where(key <= KEY_NEG_INF, -jnp.inf, pltpu.bitcast(bits, F32))


def _tree_reduce(parts, fn):
    while len(parts) > 1:
        parts = [fn(parts[j], parts[j + 1]) for j in range(0, len(parts) - 1, 2)] + (
            parts[-1:] if len(parts) % 2 else [])
    return parts[0]


def _kth_largest(count_ge, k, n_all, bits_per_step=1):
    kf = float(k)
    zero_key = jnp.zeros(n_all.shape, I32)
    c0 = count_ge(_threshold(zero_key))
    nonneg = c0 >= kf
    state = (jnp.where(nonneg, zero_key, INT_MIN), jnp.where(nonneg, c0, n_all))

    def step(state, shift, n_bits):
        prefix, n_ge = state
        new_prefix, new_ge = prefix, n_ge
        for j in range(1, 2 ** n_bits):
            cand = prefix + lax.shift_left(jnp.int32(j), shift)
            c = count_ge(_threshold(cand))
            keep = c >= kf
            new_prefix, new_ge = jnp.where(keep, cand, new_prefix), jnp.where(keep, c, new_ge)
        return new_prefix, new_ge

    low_bits = 31
    n_steps, rest = divmod(low_bits, bits_per_step)
    state = lax.fori_loop(
        0, n_steps, lambda it, st: step(st, low_bits - bits_per_step * (it + 1), bits_per_step), state)
    if rest:
        state = step(state, 0, rest)
    return _threshold(state[0]), state[1]


def _tie_cut(count_tie_below, need, n_bits):
    def body(it, j):
        cand = j + lax.shift_left(jnp.int32(1), n_bits - 1 - it)
        return jnp.where(count_tie_below(cand) < need, cand, j)

    return lax.fori_loop(0, n_bits, body, jnp.zeros(need.shape, I32))


def _stage_major(n_items, stages):
    for stage in stages:
        for item in range(n_items):
            stage(item)


def _attn_scratch(tq, ck, q_width):
    cols = GROUP * tq
    return [pltpu.VMEM((N_KV_HEADS, cols, q_width), BF16), pltpu.VMEM((2, N_KV_HEADS, ck, cols), F32),
            pltpu.VMEM((N_KV_HEADS, 1, cols), F32), pltpu.VMEM((N_KV_HEADS, 1, cols), F32),
            pltpu.VMEM((N_KV_HEADS, LANES, cols), F32)]


def _attn_init(m_scr, l_scr, acc_scr):
    m_scr[...] = jnp.full(m_scr.shape, NEG, F32)
    l_scr[...] = jnp.zeros(l_scr.shape, F32)
    acc_scr[...] = jnp.zeros(acc_scr.shape, F32)


def _attn_chunks(n_ck, raw_logits, finish_logits, vt_fn, s_scr, m_scr, l_scr, acc_scr):
    def produce(c, slot):
        for kv in range(N_KV_HEADS):
            s_scr[slot, kv] = raw_logits(c, kv)

    def consume(c, slot, last):
        p_all, a_all = {}, {}

        def softmax(kv):
            s = finish_logits(c, s_scr[slot, kv], last)
            m_old = m_scr[kv]
            mn = jnp.maximum(m_old, jnp.max(s, axis=0, keepdims=True))
            a_all[kv] = jnp.exp2(m_old - mn)
            p = jnp.exp2(s - mn)
            l_scr[kv] = a_all[kv] * l_scr[kv] + jnp.sum(p, axis=0, keepdims=True)
            m_scr[kv] = mn
            p_all[kv] = p.astype(BF16)

        def weighted_values(kv):
            acc_scr[kv] = a_all.pop(kv) * acc_scr[kv] + _dot(vt_fn(c, kv), p_all.pop(kv))

        _stage_major(N_KV_HEADS, (softmax, weighted_values))

    produce(0, 0)
    n_pairs = (n_ck - 1) // 2

    def body(j, carry):
        c = 2 * j
        produce(c + 1, 1)
        consume(c, 0, False)
        produce(c + 2, 0)
        consume(c + 1, 1, False)
        return carry

    lax.fori_loop(0, n_pairs, body, 0)
    c = 2 * n_pairs
    left = n_ck - 1 - c

    @pl.when(left == 0)
    def _():
        consume(c, 0, True)

    @pl.when(left == 1)
    def _():
        produce(c + 1, 1)
        consume(c, 0, False)
        consume(c + 1, 1, True)


def _attn_finish(o_ref, gz_ref, l_scr, acc_scr, tq):
    for kv in range(N_KV_HEADS):
        o = (acc_scr[kv] / l_scr[kv]).T
        for g in range(GROUP):
            c0 = (kv * GROUP + g) * LANES
            o_ref[:, c0:c0 + LANES] = (o[g * tq:(g + 1) * tq] * gz_ref[:, c0:c0 + LANES].astype(F32)).astype(BF16)


def _dsa_prompt_kernel(q_ref, qi_ref, qis_ref, wit_ref, gz_ref, k_ref, vt_ref, kia_ref, o_ref, s_scr, b_scr, qi2_scr,
                       q2_scr, ls_scr, m_scr, l_scr, acc_scr, *, tq, ck, topk, n_bits):
    i = pl.program_id(1)
    n_ck = ((i + 1) * tq + ck - 1) // ck
    kpos = lax.broadcasted_iota(I32, (ck, tq), 0)
    qpos = i * tq + lax.broadcasted_iota(I32, (ck, tq), 1)
    wit = wit_ref[...]

    n_pairs = IDX_HEADS // 2
    for p in range(n_pairs):
        qi2_scr[p, 0:tq, :] = qi_ref[:, p * LANES:(p + 1) * LANES]
        qi2_scr[p, tq:2 * tq, :] = qis_ref[:, p * LANES:(p + 1) * LANES]

    def score_chunk(c, carry):
        ka = kia_ref[pl.ds(c * ck, ck), :]
        dots = [_dot_nt(ka, qi2_scr[p]) for p in range(n_pairs)]
        acc = None
        for hh in range(IDX_HEADS):
            d = dots[hh // 2][:, (hh % 2) * tq:(hh % 2 + 1) * tq]
            term = jnp.maximum(d, 0.0) * wit[hh:hh + 1, :]
            acc = term if acc is None else acc + term
        s_scr[c] = jnp.where(c * ck + kpos <= qpos, acc, -jnp.inf)
        return carry

    lax.fori_loop(0, n_ck, score_chunk, 0)

    def count(pred):
        def body(c, acc):
            m = jnp.where(pred(s_scr[c], c * ck + kpos), 1.0, 0.0)
            return acc + _tree_reduce([m[r:r + SUBLANES] for r in range(0, ck, SUBLANES)], jnp.add)

        acc = lax.fori_loop(0, n_ck, body, jnp.zeros((SUBLANES, tq), F32))
        return jnp.sum(acc, axis=0, keepdims=True)

    n_all = jnp.zeros((1, tq), F32) + (n_ck * ck).astype(F32)
    v, n_ge = _kth_largest(lambda cand: count(lambda s, pos: s >= cand), topk, n_all)
    has_tie = jnp.max(jnp.where((n_ge > float(topk)) & (v > -jnp.inf), 1.0, 0.0)) > 0.5

    def tie_cut():
        need = float(topk) - count(lambda s, pos: s > v)
        return _tie_cut(lambda cand: count(lambda s, pos: (s == v) & (pos < cand)), need, n_bits)

    jcut = lax.cond(has_tie, tie_cut, lambda: jnp.full((1, tq), 2 ** n_bits, I32))

    def bias_chunk(c, carry):
        s = s_scr[c]
        sel = ((s > v) | ((s == v) & (c * ck + kpos <= jcut))) & (s > -jnp.inf)
        b_scr[c] = jnp.where(sel, 0.0, NEG)
        return carry

    lax.fori_loop(0, n_ck, bias_chunk, 0)

    for kv in range(N_KV_HEADS):
        for g in range(GROUP):
            h = kv * GROUP + g
            q2_scr[kv, g * tq:(g + 1) * tq, :] = q_ref[:, h * LANES:(h + 1) * LANES]
    _attn_init(m_scr, l_scr, acc_scr)

    _attn_chunks(
        n_ck,
        lambda c, kv: _dot_nt(k_ref[pl.ds(c * ck, ck), kv * LANES:(kv + 1) * LANES], q2_scr[kv]),
        lambda c, s, last: s + jnp.concatenate([b_scr[c]] * GROUP, axis=1),
        lambda c, kv: vt_ref[c, kv * LANES:(kv + 1) * LANES, :],
        ls_scr, m_scr, l_scr, acc_scr)
    _attn_finish(o_ref, gz_ref, l_scr, acc_scr, tq)


def _dsa_prompt_call(q, qi, qis, wit, gz, kb, vt, kia, *, n_batch):
    rows = q.shape[0]
    t = rows // n_batch
    tq = min(128, t)
    ck = vt.shape[2]
    n_c = t // ck
    n_q = t // tq
    topk = min(TOPK_MAX, t // 4)
    n_bits = max(1, int(math.ceil(math.log2(t))))
    qrow = lambda w: pl.BlockSpec((tq, w), lambda b, i: (b * n_q + i, 0))
    seq = lambda w: pl.BlockSpec((t, w), lambda b, i: (b, 0))
    return pl.pallas_call(
        functools.partial(_dsa_prompt_kernel, tq=tq, ck=ck, topk=topk, n_bits=n_bits),
        grid=(n_batch, n_q),
        in_specs=[qrow(ATT_W), qrow(IDX_HEADS * IDX_DIM), qrow(IDX_HEADS * IDX_DIM),
                  pl.BlockSpec((IDX_HEADS, tq), lambda b, i: (0, b * n_q + i)),
                  qrow(ATT_W), seq(KV_W), pl.BlockSpec((n_c, KV_W, ck), lambda b, i: (b, 0, 0)), seq(LANES)],
        out_specs=qrow(ATT_W),
        out_shape=jax.ShapeDtypeStruct((rows, ATT_W), BF16),
        scratch_shapes=[pltpu.VMEM((n_c, ck, tq), F32), pltpu.VMEM((n_c, ck, tq), F32),
                        pltpu.VMEM((IDX_HEADS // 2, GROUP * tq, LANES), BF16)] + _attn_scratch(tq, ck, LANES),
        compiler_params=_params(2),
        name="dsa_prompt_attn",
    )(q, qi, qis, wit, gz, kb, vt, kia)


def _fox_prompt_kernel(q_ref, gz_ref, kx_ref, vt_ref, o_ref, q2_scr, ls_scr, m_scr, l_scr, acc_scr, *, tq, ck):
    i = pl.program_id(1)
    n_ck = ((i + 1) * tq + ck - 1) // ck
    lane = lax.broadcasted_iota(I32, (tq, LANES), 1)
    for kv in range(N_KV_HEADS):
        for g in range(GROUP):
            h = kv * GROUP + g
            ones = jnp.where(((lane & (N_HEADS - 1)) == h) & (lane < BIAS_PARTS * N_HEADS), 1.0, 0.0)
            q2_scr[kv, g * tq:(g + 1) * tq, 0:LANES] = q_ref[:, h * LANES:(h + 1) * LANES]
            q2_scr[kv, g * tq:(g + 1) * tq, LANES:2 * LANES] = ones.astype(BF16)
    _attn_init(m_scr, l_scr, acc_scr)

    def causal_mask(c, s, last):
        if not last:
            return s
        key = c * ck + lax.broadcasted_iota(I32, (ck, GROUP * tq), 0)
        qpos = i * tq + (lax.broadcasted_iota(I32, (ck, GROUP * tq), 1) & (tq - 1))
        return jnp.where(key <= qpos, s, NEG)

    _attn_chunks(
        n_ck,
        lambda c, kv: _dot_nt(kx_ref[pl.ds(c * ck, ck), 2 * kv * LANES:(2 * kv + 2) * LANES], q2_scr[kv]),
        causal_mask,
        lambda c, kv: vt_ref[c, kv * LANES:(kv + 1) * LANES, :],
        ls_scr, m_scr, l_scr, acc_scr)
    _attn_finish(o_ref, gz_ref, l_scr, acc_scr, tq)


def _fox_prompt_call(q, gz, kx, vt, *, n_batch, tq):
    rows = q.shape[0]
    t = rows // n_batch
    ck = vt.shape[2]
    n_c = t // ck
    n_q = t // tq
    assert tq & (tq - 1) == 0 and BIAS_PARTS * N_HEADS <= LANES
    qrow = lambda w: pl.BlockSpec((tq, w), lambda b, i: (b * n_q + i, 0))
    return pl.pallas_call(
        functools.partial(_fox_prompt_kernel, tq=tq, ck=ck),
        grid=(n_batch, n_q),
        in_specs=[qrow(ATT_W), qrow(ATT_W), pl.BlockSpec((t, 2 * KV_W), lambda b, i: (b, 0)),
                  pl.BlockSpec((n_c, KV_W, ck), lambda b, i: (b, 0, 0))],
        scratch_shapes=_attn_scratch(tq, ck, 2 * LANES),
        out_specs=qrow(ATT_W),
        out_shape=jax.ShapeDtypeStruct((rows, ATT_W), BF16),
        compiler_params=_params(2),
        name="fox_prompt_attn",
    )(q, gz, kx, vt)


def _group_rows(x0, x1, row):
    return jnp.where(row < SAMPLE_ROWS // 2, x0, pltpu.roll(x1, SAMPLE_ROWS // 2, 0))


def _page_block(new_ref):
    new = new_ref[0]
    return jnp.concatenate([new, jnp.zeros((PAGE_SIZE - SAMPLE_ROWS, new.shape[1]), new.dtype)], axis=0)


def _kv_pages(page_refs, new_ref, kv):
    pages = [r[0, 0, pl.ds(kv, PAGE_SIZE, stride=N_KV_HEADS), :].astype(BF16) for r in page_refs]
    new = new_ref[0, pl.ds(kv, SAMPLE_ROWS, stride=N_KV_HEADS), :]
    new = jnp.concatenate([new, jnp.zeros((PAGE_SIZE - SAMPLE_ROWS, LANES), new.dtype)], axis=0)
    return pages + [new.astype(BF16)]


def _sample_attend(q_ref, gz_ref, k_refs, kn_ref, v_refs, vn_ref, bias_fn, o_ref):
    row = lax.broadcasted_iota(I32, (SAMPLE_ROWS, LANES), 0)
    real = row < SAMPLE_ROWS // 2
    qf = q_ref[0].astype(F32)
    gz = gz_ref[0].astype(F32)
    sl = lambda a, h: a[:, h * LANES:(h + 1) * LANES]
    s_all, p_all, l_all = {}, {}, {}

    def logits(kv):
        h0 = kv * GROUP
        q2 = _group_rows(sl(qf, h0), sl(qf, h0 + 1), row).astype(BF16)
        s_all[kv] = jnp.concatenate([_dot_nt(q2, kp) for kp in _kv_pages(k_refs, kn_ref, kv)], axis=1)

    def softmax(kv):
        s = s_all.pop(kv) + bias_fn(kv)
        p = jnp.exp2(s - jnp.max(s, axis=1, keepdims=True))
        l_all[kv] = jnp.sum(p, axis=1, keepdims=True)
        p_all[kv] = p.astype(BF16)

    def weighted_values(kv):
        h0 = kv * GROUP
        pb = p_all.pop(kv)
        o = _tree_reduce([_dot(pb[:, n * LANES:(n + 1) * LANES], vp)
                          for n, vp in enumerate(_kv_pages(v_refs, vn_ref, kv))], jnp.add)
        o = o / l_all.pop(kv)
        o_ref[0, :, h0 * LANES:(h0 + 1) * LANES] = jnp.where(real, o * sl(gz, h0), 0.0).astype(BF16)
        o_ref[0, :, (h0 + 1) * LANES:(h0 + 2) * LANES] = jnp.where(
            real, pltpu.roll(o, SAMPLE_ROWS // 2, 0) * sl(gz, h0 + 1), 0.0).astype(BF16)

    _stage_major(N_KV_HEADS, (logits, softmax, weighted_values))


def _dsa_sample_kernel(pt_ref, q_ref, qi_ref, wi_ref, gz_ref, kn_ref, vn_ref, kin_ref, *rest,
                       n_pages, topk, n_bits):
    k_refs, v_refs, ki_refs = rest[:n_pages], rest[n_pages:2 * n_pages], rest[2 * n_pages:3 * n_pages]
    o_ref = rest[3 * n_pages]
    n_chunks = n_pages + 1
    n_keys = n_chunks * PAGE_SIZE
    past = n_pages * PAGE_SIZE
    qi = qi_ref[0]
    wi = wi_ref[0]

    def score(dots):
        t = jnp.maximum(dots, 0.0) * wi
        return jnp.sum(t.reshape(SAMPLE_ROWS, IDX_HEADS, PAGE_SIZE), axis=1)

    chunks = [score(_dot(qi, r[0, 0].astype(BF16))) for r in ki_refs]
    chunks.append(score(_dot_nt(qi, _page_block(kin_ref).astype(BF16))))
    score_all = jnp.concatenate(chunks, axis=1)
    row = lax.broadcasted_iota(I32, (SAMPLE_ROWS, n_keys), 0)
    col = lax.broadcasted_iota(I32, (SAMPLE_ROWS, n_keys), 1)
    causal = col <= past + row
    s = jnp.where(causal, score_all, -jnp.inf)

    count = lambda m: jnp.sum(jnp.where(m, 1.0, 0.0), axis=1, keepdims=True)
    v, n_ge = _kth_largest(lambda cand: count(s >= cand), topk, jnp.full((SAMPLE_ROWS, 1), float(n_keys), F32),
                           bits_per_step=2)
    has_tie = jnp.max(jnp.where((n_ge > float(topk)) & (v > -jnp.inf), 1.0, 0.0)) > 0.5

    def tie_cut():
        need = float(topk) - count(s > v)
        return _tie_cut(lambda cand: count((s == v) & (col < cand)), need, n_bits)

    jcut = lax.cond(has_tie, tie_cut, lambda: jnp.full((SAMPLE_ROWS, 1), 2 ** n_bits, I32))
    sel = ((s > v) | ((s == v) & (col <= jcut))) & causal
    bias = jnp.where(sel, 0.0, NEG)
    bias2 = jnp.where(row < SAMPLE_ROWS // 2, bias, pltpu.roll(bias, SAMPLE_ROWS // 2, 0))

    _sample_attend(q_ref, gz_ref, k_refs, kn_ref, v_refs, vn_ref, lambda kv: bias2, o_ref)


def _fox_sample_kernel(pt_ref, q_ref, gz_ref, kn_ref, vn_ref, lfn_ref, *rest, n_pages):
    k_refs, v_refs, lf_refs = rest[:n_pages], rest[n_pages:2 * n_pages], rest[2 * n_pages:3 * n_pages]
    o_ref = rest[3 * n_pages]
    n_keys = (n_pages + 1) * PAGE_SIZE
    past = n_pages * PAGE_SIZE
    lf_t = [r[0, 0] for r in lf_refs]
    c_all = _lane_cumsum(jnp.concatenate(lf_t + [lfn_ref[0]], axis=1)) * LOG2E
    row = lax.broadcasted_iota(I32, (SAMPLE_ROWS, n_keys), 0)
    col = lax.broadcasted_iota(I32, (SAMPLE_ROWS, n_keys), 1)
    tok = row & (SAMPLE_ROWS // 2 - 1)
    mask = jnp.where(col <= past + tok, 0.0, NEG)

    def bias_fn(kv):
        h0 = kv * GROUP
        c2 = jnp.where(row < SAMPLE_ROWS // 2, c_all[h0:h0 + 1, :], c_all[h0 + 1:h0 + 2, :])
        return mask - c2

    _sample_attend(q_ref, gz_ref, k_refs, kn_ref, v_refs, vn_ref, bias_fn, o_ref)


def _page_specs(layer, n_pages, rows, width):
    def spec(p):
        return pl.BlockSpec((1, 1, rows, width), lambda b, pt: (layer, pt[b * n_pages + p], 0, 0))
    return [spec(p) for p in range(n_pages)]


def _kv_page_specs(layer, n_pages):
    return _page_specs(layer, n_pages, PAGE_SIZE * N_KV_HEADS, HEAD_DIM)


def _seq_spec(a):
    return pl.BlockSpec((1,) + a.shape[1:], lambda b, pt: (b,) + (0,) * (a.ndim - 1))


def _dsa_sample_call(pt_flat, q, qi, wi, gz, kn, vn, kin, cache_k, cache_v, cache_ki, *, layer, n_pages, dec_seq):
    n_seq = q.shape[0]
    n_keys = (n_pages + 1) * PAGE_SIZE
    topk = min(TOPK_MAX, (n_pages * PAGE_SIZE + dec_seq) // 4)
    n_bits = int(math.ceil(math.log2(n_keys)))
    seq_in = [q, qi, wi, gz, kn, vn, kin]
    grid_spec = pltpu.PrefetchScalarGridSpec(
        num_scalar_prefetch=1,
        grid=(n_seq,),
        in_specs=[_seq_spec(a) for a in seq_in]
        + _kv_page_specs(layer, n_pages) + _kv_page_specs(layer, n_pages)
        + _page_specs(layer, n_pages, IDX_DIM, PAGE_SIZE),
        out_specs=pl.BlockSpec((1, SAMPLE_ROWS, ATT_W), lambda b, pt: (b, 0, 0)),
    )
    return pl.pallas_call(
        functools.partial(_dsa_sample_kernel, n_pages=n_pages, topk=topk, n_bits=n_bits),
        grid_spec=grid_spec,
        out_shape=jax.ShapeDtypeStruct((n_seq, SAMPLE_ROWS, ATT_W), BF16),
        compiler_params=_params(1),
        name="dsa_sample_attn",
    )(pt_flat, *seq_in, *([cache_k] * n_pages), *([cache_v] * n_pages), *([cache_ki] * n_pages))


def _fox_sample_call(pt_flat, q, gz, kn, vn, lfn, cache_k, cache_v, cache_lf, *, layer, n_pages):
    n_seq = q.shape[0]
    seq_in = [q, gz, kn, vn, lfn]
    grid_spec = pltpu.PrefetchScalarGridSpec(
        num_scalar_prefetch=1,
        grid=(n_seq,),
        in_specs=[_seq_spec(a) for a in seq_in]
        + _kv_page_specs(layer, n_pages) + _kv_page_specs(layer, n_pages)
        + _page_specs(layer, n_pages, N_HEADS, PAGE_SIZE),
        out_specs=pl.BlockSpec((1, SAMPLE_ROWS, ATT_W), lambda b, pt: (b, 0, 0)),
    )
    return pl.pallas_call(
        functools.partial(_fox_sample_kernel, n_pages=n_pages),
        grid_spec=grid_spec,
        out_shape=jax.ShapeDtypeStruct((n_seq, SAMPLE_ROWS, ATT_W), BF16),
        compiler_params=_params(1),
        name="fox_sample_attn",
    )(pt_flat, *seq_in, *([cache_k] * n_pages), *([cache_v] * n_pages), *([cache_lf] * n_pages))


def _rope_tables(pos, dim, rot_dim):
    half = rot_dim // 2
    inv = ROPE_THETA ** (-jnp.arange(half, dtype=F32) / half)
    ang = pos.astype(F32)[:, None] * inv[None, :]
    cos, sin = jnp.cos(ang), jnp.sin(ang)
    n = pos.shape[0]
    cos_t = jnp.concatenate([cos, cos, jnp.ones((n, dim - rot_dim), F32)], axis=1)
    sin_t = jnp.concatenate([-sin, sin, jnp.zeros((n, dim - rot_dim), F32)], axis=1)
    rep = LANES // dim
    return jnp.tile(cos_t, (1, rep)), jnp.tile(sin_t, (1, rep))


def kernel(x_prompt, x_sample, cache_dsa_k, cache_dsa_v, cache_dsa_kidx, cache_fox_k, cache_fox_v, cache_fox_logf,
           page_table, c_prompt, c_sample, norm_g, w_ada, b_ada, w_in_dsa, w_out_dsa, w_in_fox, b_forget,
           w_out_fox, final_g):
    n_b, t_p, d = x_prompt.shape
    n_s, t_s, _ = x_sample.shape
    depth = norm_g.shape[0]
    n_pages = page_table.shape[1]
    past = n_pages * PAGE_SIZE
    n_pool = cache_dsa_k.shape[1]
    assert t_s <= SAMPLE_ROWS // 2 and d == ATT_W

    n_c = n_b + n_s
    n_cp = -(-n_c // SUBLANES) * SUBLANES
    c_all = jnp.concatenate([c_prompt, c_sample, jnp.zeros((n_cp - n_c, d), F32)], axis=0)
    mod = _ada_call(c_all, w_ada, b_ada)

    pos_p = jnp.arange(t_p)
    pos_s = jnp.tile(past + jnp.arange(SAMPLE_ROWS), n_s)
    tabs_p = _rope_tables(pos_p, HEAD_DIM, ROT_DIM) + _rope_tables(pos_p, IDX_DIM, IDX_ROT_DIM)
    tabs_s = _rope_tables(pos_s, HEAD_DIM, ROT_DIM) + _rope_tables(pos_s, IDX_DIM, IDX_ROT_DIM)

    rows_p = n_b * t_p
    rows_s = n_s * SAMPLE_ROWS
    tm_p = _row_tile(t_p)
    tpb = t_p // tm_p
    tm_s = _row_tile(rows_s)
    xp = x_prompt.reshape(rows_p, d)
    xs = jnp.pad(x_sample, ((0, 0), (0, SAMPLE_ROWS - t_s), (0, 0))).reshape(rows_s, d)
    pt_flat = page_table.reshape(-1).astype(I32)

    page_view = lambda c: c.reshape(c.shape[0], n_pool, PAGE_SIZE * N_KV_HEADS, HEAD_DIM)
    ck_dsa, cv_dsa, ck_fox, cv_fox = (page_view(c) for c in (cache_dsa_k, cache_dsa_v, cache_fox_k, cache_fox_v))
    cki_dsa = jnp.swapaxes(cache_dsa_kidx, 2, 3)
    clf_fox = jnp.swapaxes(cache_fox_logf, 2, 3)

    fg = final_g.reshape(1, d)
    tq_fox = min(128, t_p)
    outs = {k: [] for k in ("dk_p", "dv_p", "dki_p", "fk_p", "fv_p", "fl_p", "dk_s", "dv_s", "dki_s", "fk_s", "fv_s", "fl_s")}

    h3 = lambda a: a.reshape(n_s, SAMPLE_ROWS * N_KV_HEADS, HEAD_DIM)
    h4 = lambda a: a.reshape(n_s, SAMPLE_ROWS, N_KV_HEADS, HEAD_DIM)

    def per_row(a):
        return jnp.repeat(a, SAMPLE_ROWS, axis=0).reshape(rows_s // tm_s, tm_s, d)

    for i in range(depth):
        j = i // N_MIXERS
        g = norm_g[i].reshape(1, d)
        shift_p, scale_p, gate_p = (mod[i, :n_b, k * d:(k + 1) * d].reshape(n_b, 1, d) for k in range(3))
        shift_s, scale_s, gate_s = (per_row(mod[i, n_b:n_c, k * d:(k + 1) * d]) for k in range(3))
        final = i == depth - 1
        if i % N_MIXERS == 0:
            w = w_in_dsa[j]
            n_main = 2 * ATT_W + 2 * KV_W + IDX_HEADS * IDX_DIM
            w_main = w[:, :n_main].astype(BF16)
            w_tail = jnp.pad(w[:, n_main:], ((0, 0), (0, LANES - (w.shape[1] - n_main)))).astype(BF16)
            w_vt = w[:, ATT_W + KV_W:ATT_W + 2 * KV_W].T.astype(BF16)
            w_wit = jnp.pad(w[:, n_main + IDX_DIM:].T, ((0, 2 * SUBLANES - IDX_HEADS), (0, 0))).astype(BF16)
            w_out = w_out_dsa[j].astype(BF16)
            q, kf, kb, vf, vt, gz, qi, qis, kif, kia, wi, wit = _inproj_dsa_call(
                xp, scale_p, shift_p, g, tabs_p, w_main, w_tail, w_vt, w_wit, tiles_per_batch=tpb, per_row=False)
            og = _dsa_prompt_call(q, qi, qis, wit, gz, kb, vt, kia, n_batch=n_b)
            xp = _outproj_call(og, xp, gate_p, w_out, fg, tiles_per_batch=tpb, per_row=False, final=final)
            outs["dk_p"].append(kf.reshape(n_b, t_p, N_KV_HEADS, HEAD_DIM))
            outs["dv_p"].append(vf.reshape(n_b, t_p, N_KV_HEADS, HEAD_DIM))
            outs["dki_p"].append(kif.reshape(n_b, t_p, IDX_DIM))
            q, kf, kb, vf, vt, gz, qi, qis, kif, kia, wi, wit = _inproj_dsa_call(
                xs, scale_s, shift_s, g, tabs_s, w_main, w_tail, w_vt, w_wit, tiles_per_batch=1, per_row=True)
            r3 = lambda a: a.reshape(n_s, SAMPLE_ROWS, a.shape[-1])
            og = _dsa_sample_call(
                pt_flat, r3(q), qi.reshape(n_s, SAMPLE_ROWS * IDX_HEADS, IDX_DIM),
                wi.reshape(n_s, SAMPLE_ROWS * IDX_HEADS, 1), r3(gz), h3(kf), h3(vf), r3(kif),
                ck_dsa, cv_dsa, cki_dsa, layer=j, n_pages=n_pages, dec_seq=t_s)
            xs = _outproj_call(og.reshape(rows_s, ATT_W), xs, gate_s, w_out, fg, tiles_per_batch=1, per_row=True,
                               final=final)
            outs["dk_s"].append(h4(kf)[:, :t_s])
            outs["dv_s"].append(h4(vf)[:, :t_s])
            outs["dki_s"].append(r3(kif)[:, :t_s])
        else:
            w = w_in_fox[j]
            n_main = 2 * ATT_W + 2 * KV_W
            w_main = w[:, :n_main].astype(BF16)
            w_vt = w[:, ATT_W + KV_W:ATT_W + 2 * KV_W].T.astype(BF16)
            w_flt = jnp.pad(w[:, n_main:].T, ((0, 2 * SUBLANES - N_HEADS), (0, 0))).astype(BF16)
            b_f = b_forget[j].reshape(N_HEADS, 1)
            w_out = w_out_fox[j].astype(BF16)
            q, kf, kb, vf, vt, gz, lft = _inproj_fox_call(
                xp, scale_p, shift_p, g, w_main, w_vt, w_flt, b_f, tiles_per_batch=tpb, per_row=False)
            kx = _fox_prep_call(lft, kb, n_b)
            og = _fox_prompt_call(q, gz, kx, vt, n_batch=n_b, tq=tq_fox)
            xp = _outproj_call(og, xp, gate_p, w_out, fg, tiles_per_batch=tpb, per_row=False, final=final)
            outs["fk_p"].append(kf.reshape(n_b, t_p, N_KV_HEADS, HEAD_DIM))
            outs["fv_p"].append(vf.reshape(n_b, t_p, N_KV_HEADS, HEAD_DIM))
            outs["fl_p"].append(lft.T.reshape(n_b, t_p, N_HEADS))
            q, kf, kb, vf, vt, gz, lft = _inproj_fox_call(
                xs, scale_s, shift_s, g, w_main, w_vt, w_flt, b_f, tiles_per_batch=1, per_row=True)
            r3 = lambda a: a.reshape(n_s, SAMPLE_ROWS, a.shape[-1])
            lf3 = lft.reshape(N_HEADS, n_s, SAMPLE_ROWS).transpose(1, 0, 2)
            lfn = jnp.pad(lf3, ((0, 0), (0, 0), (0, PAGE_SIZE - SAMPLE_ROWS)))
            og = _fox_sample_call(pt_flat, r3(q), r3(gz), h3(kf), h3(vf), lfn, ck_fox, cv_fox, clf_fox,
                                  layer=j, n_pages=n_pages)
            xs = _outproj_call(og.reshape(rows_s, ATT_W), xs, gate_s, w_out, fg, tiles_per_batch=1, per_row=True,
                               final=final)
            outs["fk_s"].append(h4(kf)[:, :t_s])
            outs["fv_s"].append(h4(vf)[:, :t_s])
            outs["fl_s"].append(lf3.transpose(0, 2, 1)[:, :t_s])

    y_prompt = xp.reshape(n_b, t_p, d)
    y_sample = xs.reshape(n_s, SAMPLE_ROWS, d)[:, :t_s]
    st = lambda k: jnp.stack(outs[k])
    return (y_prompt, y_sample, st("dk_p"), st("dv_p"), st("dki_p"), st("fk_p"), st("fv_p"), st("fl_p"),
            st("dk_s"), st("dv_s"), st("dki_s"), st("fk_s"), st("fv_s"), st("fl_s"))
```

```python
import functools
import math

import jax
import jax.numpy as jnp
import numpy as np
from jax import lax
from jax.experimental import pallas as pl
from jax.experimental.pallas import tpu as pltpu

N_HEADS = 8
HEAD_DIM = 128
N_KV_HEADS = 4
GROUP = N_HEADS // N_KV_HEADS
ROT_DIM = HEAD_DIM // 4
ROPE_THETA = 500000.0
IDX_HEADS = 8
IDX_DIM = 64
IDX_ROT_DIM = IDX_DIM // 4
TOPK_MAX = 256
PAGE_SIZE = 128
EPS = 1e-6
ATT_W = N_HEADS * HEAD_DIM
KV_W = N_KV_HEADS * HEAD_DIM
N_MIXERS = 2

LANES = 128
SUBLANES = 8
VMEM_LIMIT = 56 * 1024 * 1024

SAMPLE_ROWS = SUBLANES
NEG = -1e30
LOG2E = math.log2(math.e)
INT_MIN = -(2 ** 31)

F32 = jnp.float32
BF16 = jnp.bfloat16
I32 = jnp.int32


def _params(n_axes):
    return pltpu.CompilerParams(dimension_semantics=("arbitrary",) * n_axes, vmem_limit_bytes=VMEM_LIMIT)


def _dot(a, b):
    return jnp.dot(a, b, preferred_element_type=F32)


def _dot_nt(a, b):
    return lax.dot_general(a, b, (((1,), (1,)), ((), ())), preferred_element_type=F32)


def _sigmoid(x):
    return 1.0 / (1.0 + jnp.exp(-x))


def _ada_kernel(c_ref, w_ref, b_ref, o_ref):
    c = c_ref[...].astype(BF16)
    w = w_ref[0].astype(BF16)
    o_ref[0] = _dot(c, w) + b_ref[0]


def _ada_call(c_all, w_ada, b_ada):
    depth, d, n3 = w_ada.shape
    m = c_all.shape[0]
    tn = d
    return pl.pallas_call(
        _ada_kernel,
        grid=(depth, n3 // tn),
        in_specs=[
            pl.BlockSpec((m, d), lambda l, j: (0, 0)),
            pl.BlockSpec((1, d, tn), lambda l, j: (l, 0, j)),
            pl.BlockSpec((1, 1, tn), lambda l, j: (l, 0, j)),
        ],
        out_specs=pl.BlockSpec((1, m, tn), lambda l, j: (l, 0, j)),
        out_shape=jax.ShapeDtypeStruct((depth, m, n3), F32),
        compiler_params=_params(2),
        name="ada_mod",
    )(c_all, w_ada, b_ada.reshape(depth, 1, n3))


def _modulated_input(x_ref, sc_ref, sh_ref, g_ref):
    x = x_ref[...]
    ms = jnp.mean(x * x, axis=-1, keepdims=True)
    y = x * lax.rsqrt(ms + EPS) * g_ref[...]
    return (y * (1.0 + sc_ref[0]) + sh_ref[0]).astype(BF16)


def _store_head(ref, head, value):
    ref[pl.ds(head, value.shape[0], stride=N_KV_HEADS), :] = value


def _head_rows_spec(tm):
    return pl.BlockSpec((tm * N_KV_HEADS, HEAD_DIM), lambda i: (i, 0))


def _rope(v, lane, cos_t, sin_t, period, half):
    vr = jnp.where((lane & (period - 1)) < half, pltpu.roll(v, LANES - half, 1), pltpu.roll(v, half, 1))
    return v * cos_t + vr * sin_t


def _inproj_dsa_kernel(x_ref, sc_ref, sh_ref, g_ref, c128_ref, s128_ref, c64_ref, s64_ref, w_ref, wt_ref,
                       wvt_ref, wwt_ref,
                       q_ref, kf_ref, kb_ref, vf_ref, vt_ref, gz_ref, qi_ref, qis_ref, kif_ref, kia_ref, wi_ref,
                       wit_ref, *, q_scale, wi_scale):
    h = _modulated_input(x_ref, sc_ref, sh_ref, g_ref)
    tm = h.shape[0]
    lane = lax.broadcasted_iota(I32, (tm, LANES), 1)
    c128, s128, c64, s64 = c128_ref[...], s128_ref[...], c64_ref[...], s64_ref[...]
    rope_h = lambda v: _rope(v, lane, c128, s128, HEAD_DIM, ROT_DIM // 2)
    rope_i = lambda v: _rope(v, lane, c64, s64, IDX_DIM, IDX_ROT_DIM // 2)
    seg = 4 * LANES
    off = 0
    for s in range(ATT_W // seg):
        r = _dot(h, w_ref[:, off:off + seg])
        for j in range(seg // LANES):
            c0 = s * seg + j * LANES
            q_ref[:, c0:c0 + LANES] = (rope_h(r[:, j * LANES:(j + 1) * LANES]) * q_scale).astype(BF16)
        off += seg
    r = _dot(h, w_ref[:, off:off + KV_W])
    for j in range(N_KV_HEADS):
        kr = rope_h(r[:, j * LANES:(j + 1) * LANES])
        _store_head(kf_ref, j, kr)
        kb_ref[:, j * LANES:(j + 1) * LANES] = kr.astype(BF16)
    off += KV_W
    r = _dot(h, w_ref[:, off:off + KV_W])
    for j in range(N_KV_HEADS):
        _store_head(vf_ref, j, r[:, j * LANES:(j + 1) * LANES])
    vt_ref[0] = _dot_nt(wvt_ref[...], h).astype(BF16)
    off += KV_W
    for s in range(ATT_W // seg):
        r = _dot(h, w_ref[:, off:off + seg])
        gz_ref[:, s * seg:(s + 1) * seg] = (r * _sigmoid(r)).astype(BF16)
        off += seg
    r = _dot(h, w_ref[:, off:off + IDX_HEADS * IDX_DIM])
    for j in range(IDX_HEADS * IDX_DIM // LANES):
        pair = rope_i(r[:, j * LANES:(j + 1) * LANES])
        qi_ref[:, j * LANES:(j + 1) * LANES] = pair.astype(BF16)
        qis_ref[:, j * LANES:(j + 1) * LANES] = pltpu.roll(pair, IDX_DIM, 1).astype(BF16)
    t = _dot(h, wt_ref[...])
    tr = rope_i(t)
    kif_ref[...] = tr[:, 0:IDX_DIM]
    kia_ref[...] = jnp.where(lane < IDX_DIM, tr, 0.0).astype(BF16)
    wi_ref[...] = t[:, IDX_DIM:IDX_DIM + IDX_HEADS] * wi_scale
    wit_ref[...] = _dot_nt(wwt_ref[...], h)[0:IDX_HEADS, :] * wi_scale


def _inproj_fox_kernel(x_ref, sc_ref, sh_ref, g_ref, w_ref, wvt_ref, wfl_ref, bf_ref,
                       q_ref, kf_ref, kb_ref, vf_ref, vt_ref, gz_ref, lft_ref, *, q_scale):
    h = _modulated_input(x_ref, sc_ref, sh_ref, g_ref)
    seg = 4 * LANES
    off = 0
    for s in range(ATT_W // seg):
        r = _dot(h, w_ref[:, off:off + seg])
        q_ref[:, s * seg:(s + 1) * seg] = (r * q_scale).astype(BF16)
        off += seg
    r = _dot(h, w_ref[:, off:off + KV_W])
    for j in range(N_KV_HEADS):
        _store_head(kf_ref, j, r[:, j * LANES:(j + 1) * LANES])
    kb_ref[...] = r.astype(BF16)
    off += KV_W
    r = _dot(h, w_ref[:, off:off + KV_W])
    for j in range(N_KV_HEADS):
        _store_head(vf_ref, j, r[:, j * LANES:(j + 1) * LANES])
    vt_ref[0] = _dot_nt(wvt_ref[...], h).astype(BF16)
    off += KV_W
    for s in range(ATT_W // seg):
        r = _dot(h, w_ref[:, off:off + seg])
        gz_ref[:, s * seg:(s + 1) * seg] = (r * _sigmoid(r)).astype(BF16)
        off += seg
    z = _dot_nt(wfl_ref[...], h)[0:N_HEADS, :] + bf_ref[...]
    lft_ref[...] = jnp.minimum(z, 0.0) - jnp.log1p(jnp.exp(-jnp.abs(z)))


def _row_tile(rows):
    return 512 if rows % 512 == 0 else rows


def _mod_specs(tm, d, tiles_per_batch, per_row):
    if per_row:
        return pl.BlockSpec((1, tm, d), lambda i: (i, 0, 0))
    return pl.BlockSpec((1, 1, d), lambda i: (i // tiles_per_batch, 0, 0))


def _inproj_dsa_call(x, scale, shift, g, tabs, w_main, w_tail, w_vt, w_wit, *, tiles_per_batch, per_row):
    rows, d = x.shape
    tm = _row_tile(rows) if per_row else rows // (scale.shape[0] * tiles_per_batch)
    n_t = tabs[0].shape[0] // tm
    row = lambda w: pl.BlockSpec((tm, w), lambda i: (i, 0))
    tab = pl.BlockSpec((tm, LANES), lambda i: (i % n_t, 0))
    mod = _mod_specs(tm, d, tiles_per_batch, per_row)
    full = lambda a: pl.BlockSpec(a.shape, lambda i: (0,) * a.ndim)
    sd = jax.ShapeDtypeStruct
    heads_f32 = (_head_rows_spec(tm), sd((rows * N_KV_HEADS, HEAD_DIM), F32))
    outs = [(row(ATT_W), sd((rows, ATT_W), BF16)), heads_f32, (row(KV_W), sd((rows, KV_W), BF16)), heads_f32,
            (pl.BlockSpec((1, KV_W, tm), lambda i: (i, 0, 0)), sd((rows // tm, KV_W, tm), BF16)),
            (row(ATT_W), sd((rows, ATT_W), BF16)),
            (row(IDX_HEADS * IDX_DIM), sd((rows, IDX_HEADS * IDX_DIM), BF16)),
            (row(IDX_HEADS * IDX_DIM), sd((rows, IDX_HEADS * IDX_DIM), BF16)), (row(IDX_DIM), sd((rows, IDX_DIM), F32)),
            (row(LANES), sd((rows, LANES), BF16)),
            (row(IDX_HEADS), sd((rows, IDX_HEADS), F32)),
            (pl.BlockSpec((IDX_HEADS, tm), lambda i: (0, i)), sd((IDX_HEADS, rows), F32))]
    return pl.pallas_call(
        functools.partial(_inproj_dsa_kernel, q_scale=HEAD_DIM ** -0.5 * LOG2E,
                          wi_scale=IDX_HEADS ** -0.5 * IDX_DIM ** -0.5),
        grid=(rows // tm,),
        in_specs=[row(d), mod, mod, full(g), tab, tab, tab, tab, full(w_main), full(w_tail), full(w_vt), full(w_wit)],
        out_specs=[s for s, _ in outs],
        out_shape=[t for _, t in outs],
        compiler_params=_params(1),
        name="inproj_dsa",
    )(x, scale, shift, g, *tabs, w_main, w_tail, w_vt, w_wit)


def _inproj_fox_call(x, scale, shift, g, w_main, w_vt, w_flt, b_f, *, tiles_per_batch, per_row):
    rows, d = x.shape
    tm = _row_tile(rows) if per_row else rows // (scale.shape[0] * tiles_per_batch)
    row = lambda w: pl.BlockSpec((tm, w), lambda i: (i, 0))
    mod = _mod_specs(tm, d, tiles_per_batch, per_row)
    full = lambda a: pl.BlockSpec(a.shape, lambda i: (0,) * a.ndim)
    sd = jax.ShapeDtypeStruct
    return pl.pallas_call(
        functools.partial(_inproj_fox_kernel, q_scale=HEAD_DIM ** -0.5 * LOG2E),
        grid=(rows // tm,),
        in_specs=[row(d), mod, mod, full(g), full(w_main), full(w_vt), full(w_flt), full(b_f)],
        out_specs=[row(ATT_W), _head_rows_spec(tm), row(KV_W), _head_rows_spec(tm),
                   pl.BlockSpec((1, KV_W, tm), lambda i: (i, 0, 0)), row(ATT_W),
                   pl.BlockSpec((N_HEADS, tm), lambda i: (0, i))],
        out_shape=[sd((rows, ATT_W), BF16), sd((rows * N_KV_HEADS, HEAD_DIM), F32), sd((rows, KV_W), BF16),
                   sd((rows * N_KV_HEADS, HEAD_DIM), F32),
                   sd((rows // tm, KV_W, tm), BF16), sd((rows, ATT_W), BF16), sd((N_HEADS, rows), F32)],
        compiler_params=_params(1),
        name="inproj_fox",
    )(x, scale, shift, g, w_main, w_vt, w_flt, b_f)


def _outproj_kernel(og_ref, x_ref, gate_ref, w_ref, fg_ref, o_ref, *, final):
    xn = x_ref[...] + gate_ref[0] * _dot(og_ref[...], w_ref[...])
    if final:
        ms = jnp.mean(xn * xn, axis=-1, keepdims=True)
        xn = xn * lax.rsqrt(ms + EPS) * fg_ref[...]
    o_ref[...] = xn


def _outproj_call(og, x, gate, w_out, final_g, *, tiles_per_batch, per_row, final):
    rows, d = x.shape
    tm = _row_tile(rows) if per_row else rows // (gate.shape[0] * tiles_per_batch)
    row = lambda w: pl.BlockSpec((tm, w), lambda i: (i, 0))
    full = lambda a: pl.BlockSpec(a.shape, lambda i: (0,) * a.ndim)
    return pl.pallas_call(
        functools.partial(_outproj_kernel, final=final),
        grid=(rows // tm,),
        in_specs=[row(ATT_W), row(d), _mod_specs(tm, d, tiles_per_batch, per_row), full(w_out), full(final_g)],
        out_specs=row(d),
        out_shape=jax.ShapeDtypeStruct((rows, d), F32),
        compiler_params=_params(1),
        name="outproj",
    )(og, x, gate, w_out, final_g)


def _lane_cumsum(x):
    n = x.shape[-1]
    lane = lax.broadcasted_iota(I32, x.shape, x.ndim - 1)
    s = 1
    while s < n:
        x = x + jnp.where(lane >= s, pltpu.roll(x, s, x.ndim - 1), 0.0)
        s *= 2
    return x


BIAS_PARTS = 3


def _fox_prep_kernel(lft_ref, kb_ref, kx_ref):
    t = lft_ref.shape[1]
    c = _lane_cumsum(lft_ref[...]) * (-LOG2E)
    ct = jnp.concatenate([c, jnp.zeros((LANES - N_HEADS, t), F32)], axis=0).T
    ext = jnp.zeros((t, LANES), F32)
    rest = ct
    for part in range(BIAS_PARTS):
        piece = rest.astype(BF16).astype(F32)
        rest = rest - piece
        ext = ext + (piece if part == 0 else pltpu.roll(piece, part * N_HEADS, 1))
    ext = ext.astype(BF16)
    for kv in range(N_KV_HEADS):
        kx_ref[:, 2 * kv * LANES:(2 * kv + 1) * LANES] = kb_ref[:, kv * LANES:(kv + 1) * LANES]
        kx_ref[:, (2 * kv + 1) * LANES:(2 * kv + 2) * LANES] = ext


def _fox_prep_call(lft, kb, n_batch):
    h, rows = lft.shape
    t = rows // n_batch
    return pl.pallas_call(
        _fox_prep_kernel,
        grid=(n_batch,),
        in_specs=[pl.BlockSpec((h, t), lambda b: (0, b)), pl.BlockSpec((t, KV_W), lambda b: (b, 0))],
        out_specs=pl.BlockSpec((t, 2 * KV_W), lambda b: (b, 0)),
        out_shape=jax.ShapeDtypeStruct((rows, 2 * KV_W), BF16),
        compiler_params=_params(1),
        name="fox_prep",
    )(lft, kb)


KEY_NEG_INF = INT_MIN + 0x7FFFFF


def _threshold(key):
    bits = key ^ ((key >> 31) & 0x7FFFFFFF)
    return jnp.where(key <= KEY_NEG_INF, -jnp.inf, pltpu.bitcast(bits, F32))


def _tree_reduce(parts, fn):
    while len(parts) > 1:
        parts = [fn(parts[j], parts[j + 1]) for j in range(0, len(parts) - 1, 2)] + (
            parts[-1:] if len(parts) % 2 else [])
    return parts[0]


def _kth_largest(count_ge, k, n_all, bits_per_step=1):
    kf = float(k)
    zero_key = jnp.zeros(n_all.shape, I32)
    c0 = count_ge(_threshold(zero_key))
    nonneg = c0 >= kf
    state = (jnp.where(nonneg, zero_key, INT_MIN), jnp.where(nonneg, c0, n_all))

    def step(state, shift, n_bits):
        prefix, n_ge = state
        new_prefix, new_ge = prefix, n_ge
        for j in range(1, 2 ** n_bits):
            cand = prefix + lax.shift_left(jnp.int32(j), shift)
            c = count_ge(_threshold(cand))
            keep = c >= kf
            new_prefix, new_ge = jnp.where(keep, cand, new_prefix), jnp.where(keep, c, new_ge)
        return new_prefix, new_ge

    low_bits = 31
    n_steps, rest = divmod(low_bits, bits_per_step)
    state = lax.fori_loop(
        0, n_steps, lambda it, st: step(st, low_bits - bits_per_step * (it + 1), bits_per_step), state)
    if rest:
        state = step(state, 0, rest)
    return _threshold(state[0]), state[1]


def _tie_cut(count_tie_below, need, n_bits):
    def body(it, j):
        cand = j + lax.shift_left(jnp.int32(1), n_bits - 1 - it)
        return jnp.where(count_tie_below(cand) < need, cand, j)

    return lax.fori_loop(0, n_bits, body, jnp.zeros(need.shape, I32))


def _stage_major(n_items, stages):
    for stage in stages:
        for item in range(n_items):
            stage(item)


def _attn_scratch(tq, ck, q_width):
    cols = GROUP * tq
    return [pltpu.VMEM((N_KV_HEADS, cols, q_width), BF16), pltpu.VMEM((2, N_KV_HEADS, ck, cols), F32),
            pltpu.VMEM((N_KV_HEADS, 1, cols), F32), pltpu.VMEM((N_KV_HEADS, 1, cols), F32),
            pltpu.VMEM((N_KV_HEADS, LANES, cols), F32)]


def _attn_init(m_scr, l_scr, acc_scr):
    m_scr[...] = jnp.full(m_scr.shape, NEG, F32)
    l_scr[...] = jnp.zeros(l_scr.shape, F32)
    acc_scr[...] = jnp.zeros(acc_scr.shape, F32)


def _attn_chunks(n_ck, raw_logits, finish_logits, vt_fn, s_scr, m_scr, l_scr, acc_scr):
    def produce(c, slot):
        for kv in range(N_KV_HEADS):
            s_scr[slot, kv] = raw_logits(c, kv)

    def consume(c, slot, last):
        p_all, a_all = {}, {}

        def softmax(kv):
            s = finish_logits(c, s_scr[slot, kv], last)
            m_old = m_scr[kv]
            mn = jnp.maximum(m_old, jnp.max(s, axis=0, keepdims=True))
            a_all[kv] = jnp.exp2(m_old - mn)
            p = jnp.exp2(s - mn)
            l_scr[kv] = a_all[kv] * l_scr[kv] + jnp.sum(p, axis=0, keepdims=True)
            m_scr[kv] = mn
            p_all[kv] = p.astype(BF16)

        def weighted_values(kv):
            acc_scr[kv] = a_all.pop(kv) * acc_scr[kv] + _dot(vt_fn(c, kv), p_all.pop(kv))

        _stage_major(N_KV_HEADS, (softmax, weighted_values))

    produce(0, 0)
    n_pairs = (n_ck - 1) // 2

    def body(j, carry):
        c = 2 * j
        produce(c + 1, 1)
        consume(c, 0, False)
        produce(c + 2, 0)
        consume(c + 1, 1, False)
        return carry

    lax.fori_loop(0, n_pairs, body, 0)
    c = 2 * n_pairs
    left = n_ck - 1 - c

    @pl.when(left == 0)
    def _():
        consume(c, 0, True)

    @pl.when(left == 1)
    def _():
        produce(c + 1, 1)
        consume(c, 0, False)
        consume(c + 1, 1, True)


def _attn_finish(o_ref, gz_ref, l_scr, acc_scr, tq):
    for kv in range(N_KV_HEADS):
        o = (acc_scr[kv] / l_scr[kv]).T
        for g in range(GROUP):
            c0 = (kv * GROUP + g) * LANES
            o_ref[:, c0:c0 + LANES] = (o[g * tq:(g + 1) * tq] * gz_ref[:, c0:c0 + LANES].astype(F32)).astype(BF16)


def _dsa_prompt_kernel(q_ref, qi_ref, qis_ref, wit_ref, gz_ref, k_ref, vt_ref, kia_ref, o_ref, s_scr, b_scr, qi2_scr,
                       q2_scr, ls_scr, m_scr, l_scr, acc_scr, *, tq, ck, topk, n_bits):
    i = pl.program_id(1)
    n_ck = ((i + 1) * tq + ck - 1) // ck
    kpos = lax.broadcasted_iota(I32, (ck, tq), 0)
    qpos = i * tq + lax.broadcasted_iota(I32, (ck, tq), 1)
    wit = wit_ref[...]

    n_pairs = IDX_HEADS // 2
    for p in range(n_pairs):
        qi2_scr[p, 0:tq, :] = qi_ref[:, p * LANES:(p + 1) * LANES]
        qi2_scr[p, tq:2 * tq, :] = qis_ref[:, p * LANES:(p + 1) * LANES]

    def score_chunk(c, carry):
        ka = kia_ref[pl.ds(c * ck, ck), :]
        dots = [_dot_nt(ka, qi2_scr[p]) for p in range(n_pairs)]
        acc = None
        for hh in range(IDX_HEADS):
            d = dots[hh // 2][:, (hh % 2) * tq:(hh % 2 + 1) * tq]
            term = jnp.maximum(d, 0.0) * wit[hh:hh + 1, :]
            acc = term if acc is None else acc + term
        s_scr[c] = jnp.where(c * ck + kpos <= qpos, acc, -jnp.inf)
        return carry

    lax.fori_loop(0, n_ck, score_chunk, 0)

    def count(pred):
        def body(c, acc):
            m = jnp.where(pred(s_scr[c], c * ck + kpos), 1.0, 0.0)
            return acc + _tree_reduce([m[r:r + SUBLANES] for r in range(0, ck, SUBLANES)], jnp.add)

        acc = lax.fori_loop(0, n_ck, body, jnp.zeros((SUBLANES, tq), F32))
        return jnp.sum(acc, axis=0, keepdims=True)

    n_all = jnp.zeros((1, tq), F32) + (n_ck * ck).astype(F32)
    v, n_ge = _kth_largest(lambda cand: count(lambda s, pos: s >= cand), topk, n_all)
    has_tie = jnp.max(jnp.where((n_ge > float(topk)) & (v > -jnp.inf), 1.0, 0.0)) > 0.5

    def tie_cut():
        need = float(topk) - count(lambda s, pos: s > v)
        return _tie_cut(lambda cand: count(lambda s, pos: (s == v) & (pos < cand)), need, n_bits)

    jcut = lax.cond(has_tie, tie_cut, lambda: jnp.full((1, tq), 2 ** n_bits, I32))

    def bias_chunk(c, carry):
        s = s_scr[c]
        sel = ((s > v) | ((s == v) & (c * ck + kpos <= jcut))) & (s > -jnp.inf)
        b_scr[c] = jnp.where(sel, 0.0, NEG)
        return carry

    lax.fori_loop(0, n_ck, bias_chunk, 0)

    for kv in range(N_KV_HEADS):
        for g in range(GROUP):
            h = kv * GROUP + g
            q2_scr[kv, g * tq:(g + 1) * tq, :] = q_ref[:, h * LANES:(h + 1) * LANES]
    _attn_init(m_scr, l_scr, acc_scr)

    _attn_chunks(
        n_ck,
        lambda c, kv: _dot_nt(k_ref[pl.ds(c * ck, ck), kv * LANES:(kv + 1) * LANES], q2_scr[kv]),
        lambda c, s, last: s + jnp.concatenate([b_scr[c]] * GROUP, axis=1),
        lambda c, kv: vt_ref[c, kv * LANES:(kv + 1) * LANES, :],
        ls_scr, m_scr, l_scr, acc_scr)
    _attn_finish(o_ref, gz_ref, l_scr, acc_scr, tq)


def _dsa_prompt_call(q, qi, qis, wit, gz, kb, vt, kia, *, n_batch):
    rows = q.shape[0]
    t = rows // n_batch
    tq = min(256, t)
    ck = vt.shape[2]
    n_c = t // ck
    n_q = t // tq
    topk = min(TOPK_MAX, t // 4)
    n_bits = max(1, int(math.ceil(math.log2(t))))
    qrow = lambda w: pl.BlockSpec((tq, w), lambda b, i: (b * n_q + i, 0))
    seq = lambda w: pl.BlockSpec((t, w), lambda b, i: (b, 0))
    return pl.pallas_call(
        functools.partial(_dsa_prompt_kernel, tq=tq, ck=ck, topk=topk, n_bits=n_bits),
        grid=(n_batch, n_q),
        in_specs=[qrow(ATT_W), qrow(IDX_HEADS * IDX_DIM), qrow(IDX_HEADS * IDX_DIM),
                  pl.BlockSpec((IDX_HEADS, tq), lambda b, i: (0, b * n_q + i)),
                  qrow(ATT_W), seq(KV_W), pl.BlockSpec((n_c, KV_W, ck), lambda b, i: (b, 0, 0)), seq(LANES)],
        out_specs=qrow(ATT_W),
        out_shape=jax.ShapeDtypeStruct((rows, ATT_W), BF16),
        scratch_shapes=[pltpu.VMEM((n_c, ck, tq), F32), pltpu.VMEM((n_c, ck, tq), F32),
                        pltpu.VMEM((IDX_HEADS // 2, GROUP * tq, LANES), BF16)] + _attn_scratch(tq, ck, LANES),
        compiler_params=_params(2),
        name="dsa_prompt_attn",
    )(q, qi, qis, wit, gz, kb, vt, kia)


def _fox_prompt_kernel(q_ref, gz_ref, kx_ref, vt_ref, o_ref, q2_scr, ls_scr, m_scr, l_scr, acc_scr, *, tq, ck):
    i = pl.program_id(1)
    n_ck = ((i + 1) * tq + ck - 1) // ck
    lane = lax.broadcasted_iota(I32, (tq, LANES), 1)
    for kv in range(N_KV_HEADS):
        for g in range(GROUP):
            h = kv * GROUP + g
            ones = jnp.where(((lane & (N_HEADS - 1)) == h) & (lane < BIAS_PARTS * N_HEADS), 1.0, 0.0)
            q2_scr[kv, g * tq:(g + 1) * tq, 0:LANES] = q_ref[:, h * LANES:(h + 1) * LANES]
            q2_scr[kv, g * tq:(g + 1) * tq, LANES:2 * LANES] = ones.astype(BF16)
    _attn_init(m_scr, l_scr, acc_scr)

    def causal_mask(c, s, last):
        if not last:
            return s
        key = c * ck + lax.broadcasted_iota(I32, (ck, GROUP * tq), 0)
        qpos = i * tq + (lax.broadcasted_iota(I32, (ck, GROUP * tq), 1) & (tq - 1))
        return jnp.where(key <= qpos, s, NEG)

    _attn_chunks(
        n_ck,
        lambda c, kv: _dot_nt(kx_ref[pl.ds(c * ck, ck), 2 * kv * LANES:(2 * kv + 2) * LANES], q2_scr[kv]),
        causal_mask,
        lambda c, kv: vt_ref[c, kv * LANES:(kv + 1) * LANES, :],
        ls_scr, m_scr, l_scr, acc_scr)
    _attn_finish(o_ref, gz_ref, l_scr, acc_scr, tq)


def _fox_prompt_call(q, gz, kx, vt, *, n_batch, tq):
    rows = q.shape[0]
    t = rows // n_batch
    ck = vt.shape[2]
    n_c = t // ck
    n_q = t // tq
    assert tq & (tq - 1) == 0 and BIAS_PARTS * N_HEADS <= LANES
    qrow = lambda w: pl.BlockSpec((tq, w), lambda b, i: (b * n_q + i, 0))
    return pl.pallas_call(
        functools.partial(_fox_prompt_kernel, tq=tq, ck=ck),
        grid=(n_batch, n_q),
        in_specs=[qrow(ATT_W), qrow(ATT_W), pl.BlockSpec((t, 2 * KV_W), lambda b, i: (b, 0)),
                  pl.BlockSpec((n_c, KV_W, ck), lambda b, i: (b, 0, 0))],
        scratch_shapes=_attn_scratch(tq, ck, 2 * LANES),
        out_specs=qrow(ATT_W),
        out_shape=jax.ShapeDtypeStruct((rows, ATT_W), BF16),
        compiler_params=_params(2),
        name="fox_prompt_attn",
    )(q, gz, kx, vt)


def _group_rows(x0, x1, row):
    return jnp.where(row < SAMPLE_ROWS // 2, x0, pltpu.roll(x1, SAMPLE_ROWS // 2, 0))


def _page_block(new_ref):
    new = new_ref[0]
    return jnp.concatenate([new, jnp.zeros((PAGE_SIZE - SAMPLE_ROWS, new.shape[1]), new.dtype)], axis=0)


def _kv_pages(page_refs, new_ref, kv):
    pages = [r[0, 0, pl.ds(kv, PAGE_SIZE, stride=N_KV_HEADS), :].astype(BF16) for r in page_refs]
    new = new_ref[0, pl.ds(kv, SAMPLE_ROWS, stride=N_KV_HEADS), :]
    new = jnp.concatenate([new, jnp.zeros((PAGE_SIZE - SAMPLE_ROWS, LANES), new.dtype)], axis=0)
    return pages + [new.astype(BF16)]


def _sample_attend(q_ref, gz_ref, k_refs, kn_ref, v_refs, vn_ref, bias_fn, o_ref):
    row = lax.broadcasted_iota(I32, (SAMPLE_ROWS, LANES), 0)
    real = row < SAMPLE_ROWS // 2
    qf = q_ref[0].astype(F32)
    gz = gz_ref[0].astype(F32)
    sl = lambda a, h: a[:, h * LANES:(h + 1) * LANES]
    s_all, p_all, l_all = {}, {}, {}

    def logits(kv):
        h0 = kv * GROUP
        q2 = _group_rows(sl(qf, h0), sl(qf, h0 + 1), row).astype(BF16)
        s_all[kv] = jnp.concatenate([_dot_nt(q2, kp) for kp in _kv_pages(k_refs, kn_ref, kv)], axis=1)

    def softmax(kv):
        s = s_all.pop(kv) + bias_fn(kv)
        p = jnp.exp2(s - jnp.max(s, axis=1, keepdims=True))
        l_all[kv] = jnp.sum(p, axis=1, keepdims=True)
        p_all[kv] = p.astype(BF16)

    def weighted_values(kv):
        h0 = kv * GROUP
        pb = p_all.pop(kv)
        o = _tree_reduce([_dot(pb[:, n * LANES:(n + 1) * LANES], vp)
                          for n, vp in enumerate(_kv_pages(v_refs, vn_ref, kv))], jnp.add)
        o = o / l_all.pop(kv)
        o_ref[0, :, h0 * LANES:(h0 + 1) * LANES] = jnp.where(real, o * sl(gz, h0), 0.0).astype(BF16)
        o_ref[0, :, (h0 + 1) * LANES:(h0 + 2) * LANES] = jnp.where(
            real, pltpu.roll(o, SAMPLE_ROWS // 2, 0) * sl(gz, h0 + 1), 0.0).astype(BF16)

    _stage_major(N_KV_HEADS, (logits, softmax, weighted_values))


def _dsa_sample_kernel(pt_ref, q_ref, qi_ref, wi_ref, gz_ref, kn_ref, vn_ref, kin_ref, *rest,
                       n_pages, topk, n_bits):
    k_refs, v_refs, ki_refs = rest[:n_pages], rest[n_pages:2 * n_pages], rest[2 * n_pages:3 * n_pages]
    o_ref = rest[3 * n_pages]
    n_chunks = n_pages + 1
    n_keys = n_chunks * PAGE_SIZE
    past = n_pages * PAGE_SIZE
    qi = qi_ref[0]
    wi = wi_ref[0]

    def score(dots):
        t = jnp.maximum(dots, 0.0) * wi
        return jnp.sum(t.reshape(SAMPLE_ROWS, IDX_HEADS, PAGE_SIZE), axis=1)

    chunks = [score(_dot(qi, r[0, 0].astype(BF16))) for r in ki_refs]
    chunks.append(score(_dot_nt(qi, _page_block(kin_ref).astype(BF16))))
    score_all = jnp.concatenate(chunks, axis=1)
    row = lax.broadcasted_iota(I32, (SAMPLE_ROWS, n_keys), 0)
    col = lax.broadcasted_iota(I32, (SAMPLE_ROWS, n_keys), 1)
    causal = col <= past + row
    s = jnp.where(causal, score_all, -jnp.inf)

    count = lambda m: jnp.sum(jnp.where(m, 1.0, 0.0), axis=1, keepdims=True)
    v, n_ge = _kth_largest(lambda cand: count(s >= cand), topk, jnp.full((SAMPLE_ROWS, 1), float(n_keys), F32),
                           bits_per_step=2)
    has_tie = jnp.max(jnp.where((n_ge > float(topk)) & (v > -jnp.inf), 1.0, 0.0)) > 0.5

    def tie_cut():
        need = float(topk) - count(s > v)
        return _tie_cut(lambda cand: count((s == v) & (col < cand)), need, n_bits)

    jcut = lax.cond(has_tie, tie_cut, lambda: jnp.full((SAMPLE_ROWS, 1), 2 ** n_bits, I32))
    sel = ((s > v) | ((s == v) & (col <= jcut))) & causal
    bias = jnp.where(sel, 0.0, NEG)
    bias2 = jnp.where(row < SAMPLE_ROWS // 2, bias, pltpu.roll(bias, SAMPLE_ROWS // 2, 0))

    _sample_attend(q_ref, gz_ref, k_refs, kn_ref, v_refs, vn_ref, lambda kv: bias2, o_ref)


def _fox_sample_kernel(pt_ref, q_ref, gz_ref, kn_ref, vn_ref, lfn_ref, *rest, n_pages):
    k_refs, v_refs, lf_refs = rest[:n_pages], rest[n_pages:2 * n_pages], rest[2 * n_pages:3 * n_pages]
    o_ref = rest[3 * n_pages]
    n_keys = (n_pages + 1) * PAGE_SIZE
    past = n_pages * PAGE_SIZE
    lf_t = [r[0, 0] for r in lf_refs]
    c_all = _lane_cumsum(jnp.concatenate(lf_t + [lfn_ref[0]], axis=1)) * LOG2E
    row = lax.broadcasted_iota(I32, (SAMPLE_ROWS, n_keys), 0)
    col = lax.broadcasted_iota(I32, (SAMPLE_ROWS, n_keys), 1)
    tok = row & (SAMPLE_ROWS // 2 - 1)
    mask = jnp.where(col <= past + tok, 0.0, NEG)

    def bias_fn(kv):
        h0 = kv * GROUP
        c2 = jnp.where(row < SAMPLE_ROWS // 2, c_all[h0:h0 + 1, :], c_all[h0 + 1:h0 + 2, :])
        return mask - c2

    _sample_attend(q_ref, gz_ref, k_refs, kn_ref, v_refs, vn_ref, bias_fn, o_ref)


def _page_specs(layer, n_pages, rows, width):
    def spec(p):
        return pl.BlockSpec((1, 1, rows, width), lambda b, pt: (layer, pt[b * n_pages + p], 0, 0))
    return [spec(p) for p in range(n_pages)]


def _kv_page_specs(layer, n_pages):
    return _page_specs(layer, n_pages, PAGE_SIZE * N_KV_HEADS, HEAD_DIM)


def _seq_spec(a):
    return pl.BlockSpec((1,) + a.shape[1:], lambda b, pt: (b,) + (0,) * (a.ndim - 1))


def _dsa_sample_call(pt_flat, q, qi, wi, gz, kn, vn, kin, cache_k, cache_v, cache_ki, *, layer, n_pages, dec_seq):
    n_seq = q.shape[0]
    n_keys = (n_pages + 1) * PAGE_SIZE
    topk = min(TOPK_MAX, (n_pages * PAGE_SIZE + dec_seq) // 4)
    n_bits = int(math.ceil(math.log2(n_keys)))
    seq_in = [q, qi, wi, gz, kn, vn, kin]
    grid_spec = pltpu.PrefetchScalarGridSpec(
        num_scalar_prefetch=1,
        grid=(n_seq,),
        in_specs=[_seq_spec(a) for a in seq_in]
        + _kv_page_specs(layer, n_pages) + _kv_page_specs(layer, n_pages)
        + _page_specs(layer, n_pages, IDX_DIM, PAGE_SIZE),
        out_specs=pl.BlockSpec((1, SAMPLE_ROWS, ATT_W), lambda b, pt: (b, 0, 0)),
    )
    return pl.pallas_call(
        functools.partial(_dsa_sample_kernel, n_pages=n_pages, topk=topk, n_bits=n_bits),
        grid_spec=grid_spec,
        out_shape=jax.ShapeDtypeStruct((n_seq, SAMPLE_ROWS, ATT_W), BF16),
        compiler_params=_params(1),
        name="dsa_sample_attn",
    )(pt_flat, *seq_in, *([cache_k] * n_pages), *([cache_v] * n_pages), *([cache_ki] * n_pages))


def _fox_sample_call(pt_flat, q, gz, kn, vn, lfn, cache_k, cache_v, cache_lf, *, layer, n_pages):
    n_seq = q.shape[0]
    seq_in = [q, gz, kn, vn, lfn]
    grid_spec = pltpu.PrefetchScalarGridSpec(
        num_scalar_prefetch=1,
        grid=(n_seq,),
        in_specs=[_seq_spec(a) for a in seq_in]
        + _kv_page_specs(layer, n_pages) + _kv_page_specs(layer, n_pages)
        + _page_specs(layer, n_pages, N_HEADS, PAGE_SIZE),
        out_specs=pl.BlockSpec((1, SAMPLE_ROWS, ATT_W), lambda b, pt: (b, 0, 0)),
    )
    return pl.pallas_call(
        functools.partial(_fox_sample_kernel, n_pages=n_pages),
        grid_spec=grid_spec,
        out_shape=jax.ShapeDtypeStruct((n_seq, SAMPLE_ROWS, ATT_W), BF16),
        compiler_params=_params(1),
        name="fox_sample_attn",
    )(pt_flat, *seq_in, *([cache_k] * n_pages), *([cache_v] * n_pages), *([cache_lf] * n_pages))


def _rope_tables(pos, dim, rot_dim):
    half = rot_dim // 2
    inv = ROPE_THETA ** (-jnp.arange(half, dtype=F32) / half)
    ang = pos.astype(F32)[:, None] * inv[None, :]
    cos, sin = jnp.cos(ang), jnp.sin(ang)
    n = pos.shape[0]
    cos_t = jnp.concatenate([cos, cos, jnp.ones((n, dim - rot_dim), F32)], axis=1)
    sin_t = jnp.concatenate([-sin, sin, jnp.zeros((n, dim - rot_dim), F32)], axis=1)
    rep = LANES // dim
    return jnp.tile(cos_t, (1, rep)), jnp.tile(sin_t, (1, rep))


def kernel(x_prompt, x_sample, cache_dsa_k, cache_dsa_v, cache_dsa_kidx, cache_fox_k, cache_fox_v, cache_fox_logf,
           page_table, c_prompt, c_sample, norm_g, w_ada, b_ada, w_in_dsa, w_out_dsa, w_in_fox, b_forget,
           w_out_fox, final_g):
    n_b, t_p, d = x_prompt.shape
    n_s, t_s, _ = x_sample.shape
    depth = norm_g.shape[0]
    n_pages = page_table.shape[1]
    past = n_pages * PAGE_SIZE
    n_pool = cache_dsa_k.shape[1]
    assert t_s <= SAMPLE_ROWS // 2 and d == ATT_W

    n_c = n_b + n_s
    n_cp = -(-n_c // SUBLANES) * SUBLANES
    c_all = jnp.concatenate([c_prompt, c_sample, jnp.zeros((n_cp - n_c, d), F32)], axis=0)
    mod = _ada_call(c_all, w_ada, b_ada)

    pos_p = jnp.arange(t_p)
    pos_s = jnp.tile(past + jnp.arange(SAMPLE_ROWS), n_s)
    tabs_p = _rope_tables(pos_p, HEAD_DIM, ROT_DIM) + _rope_tables(pos_p, IDX_DIM, IDX_ROT_DIM)
    tabs_s = _rope_tables(pos_s, HEAD_DIM, ROT_DIM) + _rope_tables(pos_s, IDX_DIM, IDX_ROT_DIM)

    rows_p = n_b * t_p
    rows_s = n_s * SAMPLE_ROWS
    tm_p = _row_tile(t_p)
    tpb = t_p // tm_p
    tm_s = _row_tile(rows_s)
    xp = x_prompt.reshape(rows_p, d)
    xs = jnp.pad(x_sample, ((0, 0), (0, SAMPLE_ROWS - t_s), (0, 0))).reshape(rows_s, d)
    pt_flat = page_table.reshape(-1).astype(I32)

    page_view = lambda c: c.reshape(c.shape[0], n_pool, PAGE_SIZE * N_KV_HEADS, HEAD_DIM)
    ck_dsa, cv_dsa, ck_fox, cv_fox = (page_view(c) for c in (cache_dsa_k, cache_dsa_v, cache_fox_k, cache_fox_v))
    cki_dsa = jnp.swapaxes(cache_dsa_kidx, 2, 3)
    clf_fox = jnp.swapaxes(cache_fox_logf, 2, 3)

    fg = final_g.reshape(1, d)
    tq_fox = min(256, t_p)
    outs = {k: [] for k in ("dk_p", "dv_p", "dki_p", "fk_p", "fv_p", "fl_p", "dk_s", "dv_s", "dki_s", "fk_s", "fv_s", "fl_s")}

    h3 = lambda a: a.reshape(n_s, SAMPLE_ROWS * N_KV_HEADS, HEAD_DIM)
    h4 = lambda a: a.reshape(n_s, SAMPLE_ROWS, N_KV_HEADS, HEAD_DIM)

    def per_row(a):
        return jnp.repeat(a, SAMPLE_ROWS, axis=0).reshape(rows_s // tm_s, tm_s, d)

    for i in range(depth):
        j = i // N_MIXERS
        g = norm_g[i].reshape(1, d)
        shift_p, scale_p, gate_p = (mod[i, :n_b, k * d:(k + 1) * d].reshape(n_b, 1, d) for k in range(3))
        shift_s, scale_s, gate_s = (per_row(mod[i, n_b:n_c, k * d:(k + 1) * d]) for k in range(3))
        final = i == depth - 1
        if i % N_MIXERS == 0:
            w = w_in_dsa[j]
            n_main = 2 * ATT_W + 2 * KV_W + IDX_HEADS * IDX_DIM
            w_main = w[:, :n_main].astype(BF16)
            w_tail = jnp.pad(w[:, n_main:], ((0, 0), (0, LANES - (w.shape[1] - n_main)))).astype(BF16)
            w_vt = w[:, ATT_W + KV_W:ATT_W + 2 * KV_W].T.astype(BF16)
            w_wit = jnp.pad(w[:, n_main + IDX_DIM:].T, ((0, 2 * SUBLANES - IDX_HEADS), (0, 0))).astype(BF16)
            w_out = w_out_dsa[j].astype(BF16)
            q, kf, kb, vf, vt, gz, qi, qis, kif, kia, wi, wit = _inproj_dsa_call(
                xp, scale_p, shift_p, g, tabs_p, w_main, w_tail, w_vt, w_wit, tiles_per_batch=tpb, per_row=False)
            og = _dsa_prompt_call(q, qi, qis, wit, gz, kb, vt, kia, n_batch=n_b)
            xp = _outproj_call(og, xp, gate_p, w_out, fg, tiles_per_batch=tpb, per_row=False, final=final)
            outs["dk_p"].append(kf.reshape(n_b, t_p, N_KV_HEADS, HEAD_DIM))
            outs["dv_p"].append(vf.reshape(n_b, t_p, N_KV_HEADS, HEAD_DIM))
            outs["dki_p"].append(kif.reshape(n_b, t_p, IDX_DIM))
            q, kf, kb, vf, vt, gz, qi, qis, kif, kia, wi, wit = _inproj_dsa_call(
                xs, scale_s, shift_s, g, tabs_s, w_main, w_tail, w_vt, w_wit, tiles_per_batch=1, per_row=True)
            r3 = lambda a: a.reshape(n_s, SAMPLE_ROWS, a.shape[-1])
            og = _dsa_sample_call(
                pt_flat, r3(q), qi.reshape(n_s, SAMPLE_ROWS * IDX_HEADS, IDX_DIM),
                wi.reshape(n_s, SAMPLE_ROWS * IDX_HEADS, 1), r3(gz), h3(kf), h3(vf), r3(kif),
                ck_dsa, cv_dsa, cki_dsa, layer=j, n_pages=n_pages, dec_seq=t_s)
            xs = _outproj_call(og.reshape(rows_s, ATT_W), xs, gate_s, w_out, fg, tiles_per_batch=1, per_row=True,
                               final=final)
            outs["dk_s"].append(h4(kf)[:, :t_s])
            outs["dv_s"].append(h4(vf)[:, :t_s])
            outs["dki_s"].append(r3(kif)[:, :t_s])
        else:
            w = w_in_fox[j]
            n_main = 2 * ATT_W + 2 * KV_W
            w_main = w[:, :n_main].astype(BF16)
            w_vt = w[:, ATT_W + KV_W:ATT_W + 2 * KV_W].T.astype(BF16)
            w_flt = jnp.pad(w[:, n_main:].T, ((0, 2 * SUBLANES - N_HEADS), (0, 0))).astype(BF16)
            b_f = b_forget[j].reshape(N_HEADS, 1)
            w_out = w_out_fox[j].astype(BF16)
            q, kf, kb, vf, vt, gz, lft = _inproj_fox_call(
                xp, scale_p, shift_p, g, w_main, w_vt, w_flt, b_f, tiles_per_batch=tpb, per_row=False)
            kx = _fox_prep_call(lft, kb, n_b)
            og = _fox_prompt_call(q, gz, kx, vt, n_batch=n_b, tq=tq_fox)
            xp = _outproj_call(og, xp, gate_p, w_out, fg, tiles_per_batch=tpb, per_row=False, final=final)
            outs["fk_p"].append(kf.reshape(n_b, t_p, N_KV_HEADS, HEAD_DIM))
            outs["fv_p"].append(vf.reshape(n_b, t_p, N_KV_HEADS, HEAD_DIM))
            outs["fl_p"].append(lft.T.reshape(n_b, t_p, N_HEADS))
            q, kf, kb, vf, vt, gz, lft = _inproj_fox_call(
                xs, scale_s, shift_s, g, w_main, w_vt, w_flt, b_f, tiles_per_batch=1, per_row=True)
            r3 = lambda a: a.reshape(n_s, SAMPLE_ROWS, a.shape[-1])
            lf3 = lft.reshape(N_HEADS, n_s, SAMPLE_ROWS).transpose(1, 0, 2)
            lfn = jnp.pad(lf3, ((0, 0), (0, 0), (0, PAGE_SIZE - SAMPLE_ROWS)))
            og = _fox_sample_call(pt_flat, r3(q), r3(gz), h3(kf), h3(vf), lfn, ck_fox, cv_fox, clf_fox,
                                  layer=j, n_pages=n_pages)
            xs = _outproj_call(og.reshape(rows_s, ATT_W), xs, gate_s, w_out, fg, tiles_per_batch=1, per_row=True,
                               final=final)
            outs["fk_s"].append(h4(kf)[:, :t_s])
            outs["fv_s"].append(h4(vf)[:, :t_s])
            outs["fl_s"].append(lf3.transpose(0, 2, 1)[:, :t_s])

    y_prompt = xp.reshape(n_b, t_p, d)
    y_sample = xs.reshape(n_s, SAMPLE_ROWS, d)[:, :t_s]
    st = lambda k: jnp.stack(outs[k])
    return (y_prompt, y_sample, st("dk_p"), st("dv_p"), st("dki_p"), st("fk_p"), st("fv_p"), st("fl_p"),
            st("dk_s"), st("dv_s"), st("dki_s"), st("fk_s"), st("fv_s"), st("fl_s"))
```
